```python
import jax, jax.numpy as jnp
from jax import lax
import numpy as np

D_MODEL = 4096
BATCH = 4
SEQ = 2048
DEPTH = 4
DEC_BATCH = 128
DEC_SEQ = 1
PAST_LEN = 8192
PAGE_SIZE = 128

N_HEADS = 32
Q_RANK = 768
KV_RANK = 256
NOPE_DIM = 128
ROPE_DIM = 64
V_DIM = 128
ROPE_THETA = 10000.0
Q_BLOCK = 128
CHUNK = 128
N_SGU_GROUPS = 16
SGU_GROUP = 128
D_SGU = N_SGU_GROUPS * SGU_GROUP
N_EXPERTS = 32
N_EXPERT_GROUPS = 4
EXPERTS_PER_GROUP = N_EXPERTS // N_EXPERT_GROUPS
TOP_K = 2
D_EXPERT = 1024
DN_ALPHA = (2 * DEPTH) ** 0.25
DN_BETA = (8 * DEPTH) ** -0.25
LN_EPS = 1e-5
RMS_EPS = 1e-6
N_IN = Q_RANK + KV_RANK + ROPE_DIM + 2 * D_SGU + 2 * D_MODEL

kernel_name = "hybrid_mla_sgu_grouped_moe_deepnorm_step"


def layer_norm(x, g, b):
    xf = x.astype(jnp.float32)
    mu = jnp.mean(xf, axis=-1, keepdims=True)
    var = jnp.mean(jnp.square(xf - mu), axis=-1, keepdims=True)
    return ((xf - mu) * lax.rsqrt(var + LN_EPS) * g + b).astype(x.dtype)


def rms_norm(x, g):
    xf = x.astype(jnp.float32)
    return (xf * lax.rsqrt(jnp.mean(xf * xf, axis=-1, keepdims=True) + RMS_EPS) * g).astype(x.dtype)


def rope(x, pos):
    half = ROPE_DIM // 2
    inv = ROPE_THETA ** (-jnp.arange(half, dtype=jnp.float32) / half)
    ang = pos.astype(jnp.float32)[:, None] * inv[None, :]
    ang = ang.reshape(ang.shape[:1] + (1,) * (x.ndim - 3) + (half,))
    cos, sin = jnp.cos(ang), jnp.sin(ang)
    xf = x.astype(jnp.float32)
    x1, x2 = xf[..., :half], xf[..., half:]
    return jnp.concatenate([x1 * cos - x2 * sin, x1 * sin + x2 * cos], axis=-1).astype(x.dtype)


def mla_attend(q_lat, q_rope, c_kv, k_rope, q_pos, k_pos):
    B, Sq, H, R = q_lat.shape
    qb = Q_BLOCK if Sq % Q_BLOCK == 0 else Sq
    nb = Sq // qb
    scale = (NOPE_DIM + ROPE_DIM) ** -0.5

    def block(args):
        ql, qr, qp = args
        s = (jnp.einsum('bqhr,bkr->bhqk', ql, c_kv, preferred_element_type=jnp.float32)
             + jnp.einsum('bqhd,bkd->bhqk', qr, k_rope, preferred_element_type=jnp.float32)) * scale
        s = jnp.where(k_pos[None, None, None, :] <= qp[None, None, :, None], s, -jnp.inf)
        p = jax.nn.softmax(s, axis=-1).astype(c_kv.dtype)
        return jnp.einsum('bhqk,bkr->bqhr', p, c_kv)

    ql = jnp.moveaxis(q_lat.reshape(B, nb, qb, H, R), 1, 0)
    qr = jnp.moveaxis(q_rope.reshape(B, nb, qb, H, ROPE_DIM), 1, 0)
    qp = q_pos.reshape(nb, qb)
    out = lax.map(block, (ql, qr, qp))
    return jnp.moveaxis(out, 0, 1).reshape(B, Sq, H, R)


def sgu_mix(v, w_s, b_s):
    B, S, _ = v.shape
    L = CHUNK if S % CHUNK == 0 else S
    n = S // L
    w = (w_s * jnp.tril(jnp.ones((CHUNK, CHUNK), w_s.dtype)))[:, :L, :L]
    vc = v.reshape(B, n, L, N_SGU_GROUPS, SGU_GROUP)
    out = jnp.einsum('gts,bnsgc->bntgc', w, vc) + b_s[:, :L].T[None, None, :, :, None]
    return out.reshape(B, S, D_SGU)


def token_mixer(x, pos, past_c, past_r, w_in, q_norm_g, kv_norm_g, w_qb, w_kvb, w_oa,
                sgu_ln_g, sgu_ln_b, w_s, b_s, w_ob, w_out):
    B, S, _ = x.shape
    cuts = np.cumsum([Q_RANK, KV_RANK, ROPE_DIM, D_SGU, D_SGU, D_MODEL]).tolist()
    z = x @ w_in
    q_a, c_a, kr_a, u_in, v_in, g_a, g_b = jnp.split(z, cuts, axis=-1)
    q = (rms_norm(q_a, q_norm_g) @ w_qb).reshape(B, S, N_HEADS, NOPE_DIM + ROPE_DIM)
    q_rope = rope(q[..., NOPE_DIM:], pos)
    c_kv = rms_norm(c_a, kv_norm_g)
    k_rope = rope(kr_a, pos)
    q_lat = jnp.einsum('bshd,rhd->bshr', q[..., :NOPE_DIM], w_kvb[..., :NOPE_DIM])
    if past_c is None:
        keys_c, keys_r, k_pos = c_kv, k_rope, pos
    else:
        keys_c = jnp.concatenate([past_c, c_kv], axis=1)
        keys_r = jnp.concatenate([past_r, k_rope], axis=1)
        k_pos = jnp.arange(past_c.shape[1] + S, dtype=jnp.int32)
    o_lat = mla_attend(q_lat, q_rope, keys_c, keys_r, pos, k_pos)
    o = jnp.einsum('bshr,rhd->bshd', o_lat, w_kvb[..., NOPE_DIM:]).reshape(B, S, N_HEADS * V_DIM)
    y_a = o @ w_oa
    u = jax.nn.gelu(u_in)
    v = layer_norm(jax.nn.gelu(v_in), sgu_ln_g, sgu_ln_b)
    y_b = (u * sgu_mix(v, w_s, b_s)) @ w_ob
    h = jax.nn.sigmoid(g_a) * y_a + jax.nn.sigmoid(g_b) * y_b
    return h @ w_out, c_kv, k_rope, v


def grouped_moe(x, w_router, b_router, w_eg, w_eu, w_ed):
    B, S, D = x.shape
    xt = x.reshape(B * S, D)
    scores = jax.nn.sigmoid(jnp.dot(xt, w_router, preferred_element_type=jnp.float32))
    sel = scores + b_router.astype(jnp.float32)
    grp = lax.top_k(sel.reshape(-1, N_EXPERT_GROUPS, EXPERTS_PER_GROUP), TOP_K)[0].sum(-1)
    gmask = jax.nn.one_hot(jnp.argmax(grp, axis=-1), N_EXPERT_GROUPS, dtype=jnp.bool_)
    masked = jnp.where(jnp.repeat(gmask, EXPERTS_PER_GROUP, axis=-1), sel, -jnp.inf)
    _, e_idx = lax.top_k(masked, TOP_K)
    w_sel = jnp.take_along_axis(scores, e_idx, axis=-1)
    w_sel = w_sel / jnp.sum(w_sel, axis=-1, keepdims=True)
    gates = jnp.sum(jax.nn.one_hot(e_idx, N_EXPERTS, dtype=jnp.float32) * w_sel[..., None], axis=1)
    gates = gates.astype(xt.dtype)
    y = jnp.zeros_like(xt)
    for e in range(N_EXPERTS):
        h = jax.nn.silu(xt @ w_eg[e]) * (xt @ w_eu[e])
        y = y + gates[:, e:e + 1] * (h @ w_ed[e])
    return y.reshape(B, S, D)


def decoder_layer(x, pos, past_c, past_r, w_router, b_router, w_in, q_norm_g, kv_norm_g, w_qb, w_kvb,
                  w_oa, sgu_ln_g, sgu_ln_b, w_s, b_s, w_ob, w_out, ln1_g, ln1_b, w_eg, w_eu, w_ed,
                  ln2_g, ln2_b):
    mix, c_kv, k_rope, v = token_mixer(x, pos, past_c, past_r, w_in, q_norm_g, kv_norm_g, w_qb, w_kvb,
                                       w_oa, sgu_ln_g, sgu_ln_b, w_s, b_s, w_ob, w_out)
    x = layer_norm(DN_ALPHA * x + mix, ln1_g, ln1_b)
    x = layer_norm(DN_ALPHA * x + grouped_moe(x, w_router, b_router, w_eg, w_eu, w_ed), ln2_g, ln2_b)
    return x, c_kv, k_rope, v


def setup_inputs(seed: int = 0) -> dict:
    key = jax.random.key(seed)
    ks = jax.random.split(key, 32)
    n_pages = PAST_LEN // PAGE_SIZE
    n_used = DEC_BATCH * n_pages
    n_pool = n_used + n_used // 4
    nrm = lambda k, shape, s: jax.random.normal(k, shape, jnp.float32) * s
    gain = lambda k, shape: 1.0 + 0.02 * jax.random.normal(k, shape, jnp.float32)
    page_table = jax.random.permutation(ks[0], n_pool)[:n_used].reshape(DEC_BATCH, n_pages).astype(jnp.int32)
    return {
        "x_prompt": nrm(ks[1], (BATCH, SEQ, D_MODEL), 1.0),
        "x_sample": nrm(ks[2], (DEC_BATCH, DEC_SEQ, D_MODEL), 1.0),
        "cache_ckv": nrm(ks[3], (DEPTH, n_pool, PAGE_SIZE, KV_RANK), 1.0),
        "cache_krope": nrm(ks[4], (DEPTH, n_pool, PAGE_SIZE, ROPE_DIM), 1.0),
        "page_table": page_table,
        "w_in": nrm(ks[5], (DEPTH, D_MODEL, N_IN), D_MODEL ** -0.5),
        "q_norm_g": gain(ks[6], (DEPTH, Q_RANK)),
        "kv_norm_g": gain(ks[7], (DEPTH, KV_RANK)),
        "w_qb": nrm(ks[8], (DEPTH, Q_RANK, N_HEADS * (NOPE_DIM + ROPE_DIM)), Q_RANK ** -0.5),
        "w_kvb": nrm(ks[9], (DEPTH, KV_RANK, N_HEADS, NOPE_DIM + V_DIM), KV_RANK ** -0.5),
        "w_oa": nrm(ks[10], (DEPTH, N_HEADS * V_DIM, D_MODEL), DN_BETA * (N_HEADS * V_DIM) ** -0.5),
        "sgu_ln_g": gain(ks[11], (DEPTH, D_SGU)),
        "sgu_ln_b": nrm(ks[12], (DEPTH, D_SGU), 0.02),
        "w_s": nrm(ks[13], (DEPTH, N_SGU_GROUPS, CHUNK, CHUNK), CHUNK ** -0.5),
        "b_s": gain(ks[14], (DEPTH, N_SGU_GROUPS, CHUNK)),
        "w_ob": nrm(ks[15], (DEPTH, D_SGU, D_MODEL), DN_BETA * D_SGU ** -0.5),
        "w_out": nrm(ks[16], (DEPTH, D_MODEL, D_MODEL), DN_BETA * D_MODEL ** -0.5),
        "ln1_g": gain(ks[17], (DEPTH, D_MODEL)),
        "ln1_b": nrm(ks[18], (DEPTH, D_MODEL), 0.02),
        "w_router": nrm(ks[19], (D_MODEL, N_EXPERTS), D_MODEL ** -0.5),
        "b_router": nrm(ks[20], (N_EXPERTS,), 0.01),
        "w_exp_gate": nrm(ks[21], (DEPTH, N_EXPERTS, D_MODEL, D_EXPERT), D_MODEL ** -0.5),
        "w_exp_up": nrm(ks[22], (DEPTH, N_EXPERTS, D_MODEL, D_EXPERT), D_MODEL ** -0.5),
        "w_exp_down": nrm(ks[23], (DEPTH, N_EXPERTS, D_EXPERT, D_MODEL), DN_BETA * D_EXPERT ** -0.5),
        "ln2_g": gain(ks[24], (DEPTH, D_MODEL)),
        "ln2_b": nrm(ks[25], (DEPTH, D_MODEL), 0.02),
    }


def reference(x_prompt, x_sample, cache_ckv, cache_krope, page_table, w_in, q_norm_g, kv_norm_g, w_qb,
              w_kvb, w_oa, sgu_ln_g, sgu_ln_b, w_s, b_s, w_ob, w_out, ln1_g, ln1_b, w_router, b_router,
              w_exp_gate, w_exp_up, w_exp_down, ln2_g, ln2_b):
    n_seq, n_pages = page_table.shape
    past_len = n_pages * PAGE_SIZE
    pos_p = jnp.arange(x_prompt.shape[1], dtype=jnp.int32)
    pos_d = past_len + jnp.arange(x_sample.shape[1], dtype=jnp.int32)
    xp, xd = x_prompt, x_sample
    ckv_p, kr_p, ckv_d, kr_d, v_d = [], [], [], [], []
    for l in range(DEPTH):
        lw = (w_in[l], q_norm_g[l], kv_norm_g[l], w_qb[l], w_kvb[l], w_oa[l], sgu_ln_g[l], sgu_ln_b[l],
              w_s[l], b_s[l], w_ob[l], w_out[l], ln1_g[l], ln1_b[l], w_exp_gate[l], w_exp_up[l],
              w_exp_down[l], ln2_g[l], ln2_b[l])
        xp, c_p, r_p, _ = decoder_layer(xp, pos_p, None, None, w_router, b_router, *lw)
        past_c = cache_ckv[l][page_table].reshape(n_seq, past_len, KV_RANK)
        past_r = cache_krope[l][page_table].reshape(n_seq, past_len, ROPE_DIM)
        xd, c_d, r_d, vv = decoder_layer(xd, pos_d, past_c, past_r, w_router, b_router, *lw)
        ckv_p.append(c_p); kr_p.append(r_p); ckv_d.append(c_d); kr_d.append(r_d); v_d.append(vv)
    return (xp, xd, jnp.stack(ckv_p), jnp.stack(kr_p), jnp.stack(ckv_d), jnp.stack(kr_d), jnp.stack(v_d))
```

```python
import functools

import jax
import jax.numpy as jnp
from jax import lax
from jax.experimental import pallas as pl
from jax.experimental.pallas import tpu as pltpu

F32 = jnp.float32
BF16 = jnp.bfloat16

ROPE_THETA = 10000.0
LN_EPS = 1e-5
RMS_EPS = 1e-6
N_EXPERT_GROUPS = 4

LANES = 128
SUBLANES = 8
VMEM_LIMIT_BYTES = 52 * 1024 * 1024

ROW_TILE = 640
COL_TILE = 1024
EXPERT_ROW_TILE = 128
EXPERT_COL_TILE = 512
DOWN_COL_TILE = 2048
PAGES_PER_STEP = 8
ATTN_Q_TILE = 512


def _params(semantics):
    return pltpu.CompilerParams(dimension_semantics=semantics, vmem_limit_bytes=VMEM_LIMIT_BYTES)


def _row_tile(m, target):
    best = None
    for d in range(SUBLANES, min(m, target) + 1, SUBLANES):
        if m % d == 0:
            best = d
    return best if best is not None else m


def _col_tile(n, target):
    best = None
    for d in range(LANES, min(n, target) + 1, LANES):
        if n % d == 0:
            best = d
    return best if best is not None else n


def _mm_body(*refs, n_extra, n_out, epi):
    x_ref, w_ref = refs[0], refs[1]
    extra = refs[2:2 + n_extra]
    outs = refs[2 + n_extra:2 + n_extra + n_out]
    acc = jnp.dot(x_ref[...], w_ref[...], preferred_element_type=F32)
    res = epi(acc, *[e[...] for e in extra])
    if not isinstance(res, tuple):
        res = (res,)
    for o_ref, r in zip(outs, res):
        o_ref[...] = r.astype(o_ref.dtype)


def _matmul(x, w, layer, *, tm, tn, epi, extras=(), outs, name):
    m, k = x.shape
    n = w.shape[-1]
    grid = (m // tm, n // tn)
    in_specs = [pl.BlockSpec((tm, k), lambda i, j: (i, 0)),
                pl.BlockSpec((None, k, tn), lambda i, j: (layer, 0, j))]
    in_specs += [pl.BlockSpec(bs, im) for (_, bs, im) in extras]
    out_specs = [pl.BlockSpec((tm, ct), lambda i, j: (i, j)) for (_, ct, _) in outs]
    out_shape = [jax.ShapeDtypeStruct((m, c), dt) for (c, _, dt) in outs]
    body = functools.partial(_mm_body, n_extra=len(extras), n_out=len(outs), epi=epi)
    res = pl.pallas_call(
        body, grid=grid, in_specs=in_specs, out_specs=out_specs, out_shape=out_shape,
        compiler_params=_params(("parallel", "arbitrary")), name=name,
    )(x, w, *[a for (a, _, _) in extras])
    return res


def _rope_pair(t, cs):
    tt = t * cs
    return tt + pltpu.roll(tt, LANES // 2, 1)


def _epi_small(acc, cs, gq, gkv, *, q_rank, kv_rank):
    qa = acc[:, :q_rank]
    ca = acc[:, q_rank:q_rank + kv_rank]
    t = acc[:, q_rank + kv_rank:]
    qn = qa * lax.rsqrt(jnp.mean(qa * qa, axis=-1, keepdims=True) + RMS_EPS) * gq
    ckv = ca * lax.rsqrt(jnp.mean(ca * ca, axis=-1, keepdims=True) + RMS_EPS) * gkv
    r = _rope_pair(t, cs)
    lane = lax.broadcasted_iota(jnp.int32, r.shape, 1)
    kr_pad = jnp.where(lane < LANES // 2, r, 0.0)
    return qn, ckv, r[:, :LANES // 2], kr_pad


def _epi_q(acc, cs, *, heads, scale):
    parts = []
    for h in range(heads):
        blk = acc[:, h * 2 * LANES:(h + 1) * 2 * LANES]
        parts.append(blk[:, :LANES] * scale)
        parts.append(_rope_pair(blk[:, LANES:], cs) * scale)
    return jnp.concatenate(parts, axis=1)


def _epi_gelu(acc):
    return jax.nn.gelu(acc)


def _epi_sigmoid(acc):
    return jax.nn.sigmoid(acc)


def _epi_id(acc):
    return acc


def _ln(x, g, b):
    mu = jnp.mean(x, axis=-1, keepdims=True)
    xc = x - mu
    var = jnp.mean(xc * xc, axis=-1, keepdims=True)
    return xc * lax.rsqrt(var + LN_EPS) * g + b


def _ln_rows_body(x_ref, g_ref, b_ref, o_ref):
    o_ref[...] = _ln(x_ref[...], g_ref[...], b_ref[...])


def _ln_rows(x, g, b, layer, *, tm, name):
    m, n = x.shape
    return pl.pallas_call(
        _ln_rows_body, grid=(m // tm,),
        in_specs=[pl.BlockSpec((tm, n), lambda i: (i, 0)),
                  pl.BlockSpec((None, 1, n), lambda i: (layer, 0, 0)),
                  pl.BlockSpec((None, 1, n), lambda i: (layer, 0, 0))],
        out_specs=pl.BlockSpec((tm, n), lambda i: (i, 0)),
        out_shape=jax.ShapeDtypeStruct((m, n), F32),
        compiler_params=_params(("parallel",)), name=name,
    )(x, g, b)


def _ln_residual_body(x_ref, y_ref, g_ref, b_ref, o_ref, ob_ref, *, alpha):
    r = _ln(alpha * x_ref[...] + y_ref[...], g_ref[...], b_ref[...])
    o_ref[...] = r
    ob_ref[...] = r.astype(BF16)


def _ln_residual(x, y, g, b, layer, alpha, *, tm, name):
    m, n = x.shape
    return pl.pallas_call(
        functools.partial(_ln_residual_body, alpha=alpha), grid=(m // tm,),
        in_specs=[pl.BlockSpec((tm, n), lambda i: (i, 0)),
                  pl.BlockSpec((tm, n), lambda i: (i, 0)),
                  pl.BlockSpec((None, 1, n), lambda i: (layer, 0, 0)),
                  pl.BlockSpec((None, 1, n), lambda i: (layer, 0, 0))],
        out_specs=[pl.BlockSpec((tm, n), lambda i: (i, 0)),
                   pl.BlockSpec((tm, n), lambda i: (i, 0))],
        out_shape=[jax.ShapeDtypeStruct((m, n), F32), jax.ShapeDtypeStruct((m, n), BF16)],
        compiler_params=_params(("parallel",)), name=name,
    )(x, y, g, b)


def _kv_expand_body(c_ref, kr_ref, wk_ref, wv_ref, k_ref, v_ref, *, heads):
    c = c_ref[...].astype(BF16)
    kn = jnp.dot(c, wk_ref[...], preferred_element_type=F32).astype(BF16)
    kr = kr_ref[...]
    for h in range(heads):
        k_ref[:, h * 2 * LANES:h * 2 * LANES + LANES] = kn[:, h * LANES:(h + 1) * LANES]
        k_ref[:, h * 2 * LANES + LANES:(h + 1) * 2 * LANES] = kr
    v_ref[...] = jnp.dot(c, wv_ref[...], preferred_element_type=F32).astype(BF16)


def _kv_expand(ckv, kr_pad, wk, wv, layer, n_rows, *, tm, heads_per_step, name):
    kv_rank = ckv.shape[1]
    n_heads = wk.shape[-1] // LANES
    hb = heads_per_step
    return pl.pallas_call(
        functools.partial(_kv_expand_body, heads=hb), grid=(n_rows // tm, n_heads // hb),
        in_specs=[pl.BlockSpec((tm, kv_rank), lambda i, j: (i, 0)),
                  pl.BlockSpec((tm, LANES), lambda i, j: (i, 0)),
                  pl.BlockSpec((None, kv_rank, hb * LANES), lambda i, j: (layer, 0, j)),
                  pl.BlockSpec((None, kv_rank, hb * LANES), lambda i, j: (layer, 0, j))],
        out_specs=[pl.BlockSpec((tm, hb * 2 * LANES), lambda i, j: (i, j)),
                   pl.BlockSpec((tm, hb * LANES), lambda i, j: (i, j))],
        out_shape=[jax.ShapeDtypeStruct((n_rows, n_heads * 2 * LANES), BF16),
                   jax.ShapeDtypeStruct((n_rows, n_heads * LANES), BF16)],
        compiler_params=_params(("parallel", "arbitrary")), name=name,
    )(ckv, kr_pad, wk, wv)


def _prompt_attn_body(q_ref, k_ref, v_ref, o_ref, *, tq, seq):
    nt = (((1,), (1,)), ((), ()))
    row = lax.broadcasted_iota(jnp.int32, (tq, tq), 0)
    col = lax.broadcasted_iota(jnp.int32, (tq, tq), 1)
    causal = col <= row
    for qi in range(seq // tq):
        off, kend = qi * tq, (qi + 1) * tq
        q = q_ref[off:kend, :]
        sd = lax.dot_general(q, k_ref[off:kend, :], nt, preferred_element_type=F32)
        sd = jnp.where(causal, sd, -jnp.inf)
        m = jnp.max(sd, axis=-1, keepdims=True)
        if qi > 0:
            so = lax.dot_general(q, k_ref[0:off, :], nt, preferred_element_type=F32)
            m = jnp.maximum(m, jnp.max(so, axis=-1, keepdims=True))
            po = jnp.exp(so - m)
        pd = jnp.exp(sd - m)
        l = jnp.sum(pd, axis=-1, keepdims=True)
        o = jnp.dot(pd.astype(BF16), v_ref[off:kend, :], preferred_element_type=F32)
        if qi > 0:
            l = l + jnp.sum(po, axis=-1, keepdims=True)
            o = o + jnp.dot(po.astype(BF16), v_ref[0:off, :], preferred_element_type=F32)
        o_ref[off:kend, :] = (o / l).astype(o_ref.dtype)


def _prompt_attn(q, k, v, batch, seq, *, tq, name):
    n_heads = v.shape[1] // LANES
    return pl.pallas_call(
        functools.partial(_prompt_attn_body, tq=tq, seq=seq), grid=(batch, n_heads),
        in_specs=[pl.BlockSpec((seq, 2 * LANES), lambda b, h: (b, h)),
                  pl.BlockSpec((seq, 2 * LANES), lambda b, h: (b, h)),
                  pl.BlockSpec((seq, LANES), lambda b, h: (b, h))],
        out_specs=pl.BlockSpec((seq, LANES), lambda b, h: (b, h)),
        out_shape=jax.ShapeDtypeStruct((batch * seq, n_heads * LANES), BF16),
        compiler_params=_params(("parallel", "parallel")), name=name,
    )(q, k, v)


def _headwise_body(x_ref, w_ref, o_ref, *, heads, in_stride, d_in, d_out):
    for h in range(heads):
        xh = x_ref[:, h * in_stride:h * in_stride + d_in]
        o_ref[:, h * d_out:(h + 1) * d_out] = jnp.dot(
            xh, w_ref[h], preferred_element_type=F32).astype(o_ref.dtype)


def _headwise(x, w, layer, *, in_stride, heads_per_step, name):
    m = x.shape[0]
    n_heads, d_in, d_out = w.shape[1:]
    hb = heads_per_step
    return pl.pallas_call(
        functools.partial(_headwise_body, heads=hb, in_stride=in_stride, d_in=d_in, d_out=d_out),
        grid=(n_heads // hb,),
        in_specs=[pl.BlockSpec((m, hb * in_stride), lambda j: (0, j)),
                  pl.BlockSpec((None, hb, d_in, d_out), lambda j: (layer, j, 0, 0))],
        out_specs=pl.BlockSpec((m, hb * d_out), lambda j: (0, j)),
        out_shape=jax.ShapeDtypeStruct((m, n_heads * d_out), BF16),
        compiler_params=_params(("parallel",)), name=name,
    )(x, w)


def _paged_attn_body(pt_ref, ql_ref, qr_ref, cn_ref, rn_ref, *rest, pages, rope_dim):
    del pt_ref
    c_refs = rest[:pages]
    r_refs = rest[pages:2 * pages]
    o_ref = rest[2 * pages]
    m_ref, l_ref, acc_ref = rest[2 * pages + 1:]
    step = pl.program_id(1)

    @pl.when(step == 0)
    def _():
        m_ref[...] = jnp.full(m_ref.shape, -jnp.inf, F32)
        l_ref[...] = jnp.zeros(l_ref.shape, F32)
        acc_ref[...] = jnp.zeros(acc_ref.shape, F32)

    ql = ql_ref[...]
    qr = qr_ref[...][:, :rope_dim]
    nt = (((1,), (1,)), ((), ()))
    cs, ss = [], []
    for i in range(pages):
        c = c_refs[i][...].astype(BF16)
        r = r_refs[i][...].astype(BF16)
        cs.append(c)
        ss.append(lax.dot_general(ql, c, nt, preferred_element_type=F32)
                  + lax.dot_general(qr, r, nt, preferred_element_type=F32))
    s = jnp.concatenate(ss, axis=1)
    m_prev = m_ref[...]
    m_new = jnp.maximum(m_prev, jnp.max(s, axis=-1, keepdims=True))
    alpha = jnp.exp(m_prev - m_new)
    p = jnp.exp(s - m_new)
    l_new = alpha * l_ref[...] + jnp.sum(p, axis=-1, keepdims=True)
    pv = jnp.dot(p[:, :LANES].astype(BF16), cs[0], preferred_element_type=F32)
    for i in range(1, pages):
        pv += jnp.dot(p[:, i * LANES:(i + 1) * LANES].astype(BF16), cs[i],
                      preferred_element_type=F32)
    acc_new = alpha * acc_ref[...] + pv
    m_ref[...] = m_new
    l_ref[...] = l_new
    acc_ref[...] = acc_new

    @pl.when(step == pl.num_programs(1) - 1)
    def _():
        cn = cn_ref[...]
        rn = rn_ref[...]
        s_self = (jnp.sum(ql.astype(F32) * cn, axis=-1, keepdims=True)
                  + jnp.sum(qr.astype(F32) * rn, axis=-1, keepdims=True))
        m2 = jnp.maximum(m_new, s_self)
        a2 = jnp.exp(m_new - m2)
        ps = jnp.exp(s_self - m2)
        l2 = a2 * l_new + ps
        o_ref[...] = ((a2 * acc_new + ps * cn) / l2).astype(o_ref.dtype)


def _paged_attn(page_table, q_lat, q_rope, c_new, r_new, cache_c, cache_r, layer, *, name):
    n_seq, n_heads, kv_rank = q_lat.shape
    n_pages = page_table.shape[1]
    page = cache_c.shape[2]
    rope_dim = cache_r.shape[3]
    assert page == LANES
    pg = PAGES_PER_STEP if n_pages % PAGES_PER_STEP == 0 else 1
    steps = n_pages // pg

    def page_map(i):
        return lambda b, p, pt: (layer, pt[b, p * pg + i], 0, 0)

    in_specs = [pl.BlockSpec((None, n_heads, kv_rank), lambda b, p, pt: (b, 0, 0)),
                pl.BlockSpec((None, n_heads, LANES), lambda b, p, pt: (b, 0, 0)),
                pl.BlockSpec((None, 1, kv_rank), lambda b, p, pt: (b, 0, 0)),
                pl.BlockSpec((None, 1, rope_dim), lambda b, p, pt: (b, 0, 0))]
    in_specs += [pl.BlockSpec((None, None, page, kv_rank), page_map(i)) for i in range(pg)]
    in_specs += [pl.BlockSpec((None, None, page, rope_dim), page_map(i)) for i in range(pg)]
    grid_spec = pltpu.PrefetchScalarGridSpec(
        num_scalar_prefetch=1, grid=(n_seq, steps), in_specs=in_specs,
        out_specs=pl.BlockSpec((None, n_heads, kv_rank), lambda b, p, pt: (b, 0, 0)),
        scratch_shapes=[pltpu.VMEM((n_heads, 1), F32), pltpu.VMEM((n_heads, 1), F32),
                        pltpu.VMEM((n_heads, kv_rank), F32)])
    return pl.pallas_call(
        functools.partial(_paged_attn_body, pages=pg, rope_dim=rope_dim),
        grid_spec=grid_spec,
        out_shape=jax.ShapeDtypeStruct((n_seq, n_heads, kv_rank), BF16),
        compiler_params=_params(("parallel", "arbitrary")), name=name,
    )(page_table, q_lat, q_rope, c_new, r_new, *([cache_c] * pg), *([cache_r] * pg))


def _sgu_prompt_body(u_ref, v_ref, w_ref, bt_ref, o_ref, *, groups):
    chunk = w_ref.shape[-1]
    row = lax.broadcasted_iota(jnp.int32, (chunk, chunk), 0)
    col = lax.broadcasted_iota(jnp.int32, (chunk, chunk), 1)
    tril = col <= row
    for g in range(groups):
        w = jnp.where(tril, w_ref[g], 0.0).astype(BF16)
        vg = v_ref[:, g * LANES:(g + 1) * LANES].astype(BF16)
        mix = jnp.dot(w, vg, preferred_element_type=F32) + bt_ref[:, g:g + 1]
        ug = u_ref[:, g * LANES:(g + 1) * LANES].astype(F32)
        o_ref[:, g * LANES:(g + 1) * LANES] = (ug * mix).astype(o_ref.dtype)


def _sgu_prompt(u, v, w_s, b_s_t, layer, n_rows, *, name):
    groups, chunk = w_s.shape[1], w_s.shape[2]
    d_sgu = u.shape[1]
    return pl.pallas_call(
        functools.partial(_sgu_prompt_body, groups=groups), grid=(n_rows // chunk,),
        in_specs=[pl.BlockSpec((chunk, d_sgu), lambda i: (i, 0)),
                  pl.BlockSpec((chunk, d_sgu), lambda i: (i, 0)),
                  pl.BlockSpec((None, groups, chunk, chunk), lambda i: (layer, 0, 0, 0)),
                  pl.BlockSpec((None, chunk, groups), lambda i: (layer, 0, 0))],
        out_specs=pl.BlockSpec((chunk, d_sgu), lambda i: (i, 0)),
        out_shape=jax.ShapeDtypeStruct((n_rows, d_sgu), BF16),
        compiler_params=_params(("parallel",)), name=name,
    )(u, v, w_s, b_s_t)


def _sgu_sample_body(u_ref, v_ref, w_ref, b_ref, o_ref):
    mix = v_ref[...] * w_ref[...] + b_ref[...]
    o_ref[...] = (u_ref[...].astype(F32) * mix).astype(o_ref.dtype)


def _sgu_sample(u, v, w_row, b_row, layer, *, name):
    m, d_sgu = u.shape
    return pl.pallas_call(
        _sgu_sample_body, grid=(1,),
        in_specs=[pl.BlockSpec((m, d_sgu), lambda i: (0, 0)),
                  pl.BlockSpec((m, d_sgu), lambda i: (0, 0)),
                  pl.BlockSpec((None, 1, d_sgu), lambda i: (layer, 0, 0)),
                  pl.BlockSpec((None, 1, d_sgu), lambda i: (layer, 0, 0))],
        out_specs=pl.BlockSpec((m, d_sgu), lambda i: (0, 0)),
        out_shape=jax.ShapeDtypeStruct((m, d_sgu), BF16),
        compiler_params=_params(("arbitrary",)), name=name,
    )(u, v, w_row, b_row)


def _gate_merge_body(o_ref, ub_ref, wa_ref, wb_ref, ga_ref, gb_ref, h_ref):
    ya = jnp.dot(o_ref[...], wa_ref[...], preferred_element_type=F32)
    yb = jnp.dot(ub_ref[...], wb_ref[...], preferred_element_type=F32)
    h_ref[...] = (ga_ref[...].astype(F32) * ya + gb_ref[...].astype(F32) * yb).astype(h_ref.dtype)


def _gate_merge(o, ub, w_oa, w_ob, gates, layer, *, tm, tn, name):
    m, ka = o.shape
    kb = ub.shape[1]
    n = w_oa.shape[-1]
    nj = n // tn
    return pl.pallas_call(
        _gate_merge_body, grid=(m // tm, nj),
        in_specs=[pl.BlockSpec((tm, ka), lambda i, j: (i, 0)),
                  pl.BlockSpec((tm, kb), lambda i, j: (i, 0)),
                  pl.BlockSpec((None, ka, tn), lambda i, j: (layer, 0, j)),
                  pl.BlockSpec((None, kb, tn), lambda i, j: (layer, 0, j)),
                  pl.BlockSpec((tm, tn), lambda i, j: (i, j)),
                  pl.BlockSpec((tm, tn), lambda i, j: (i, j + nj))],
        out_specs=pl.BlockSpec((tm, tn), lambda i, j: (i, j)),
        out_shape=jax.ShapeDtypeStruct((m, n), BF16),
        compiler_params=_params(("parallel", "arbitrary")), name=name,
    )(o, ub, w_oa, w_ob, gates, gates)


def _router_body(x_ref, w_ref, b_ref, e1_ref, e2_ref, g1_ref, g2_ref, *, n_experts):
    logits = jnp.dot(x_ref[...], w_ref[...].astype(BF16), preferred_element_type=F32)
    scores = jax.nn.sigmoid(logits)
    sel = scores + b_ref[...]
    per_group = n_experts // N_EXPERT_GROUPS
    lane = lax.broadcasted_iota(jnp.int32, sel.shape, 1).astype(F32)
    neg = -jnp.inf

    def top2(vals):
        m1 = jnp.max(vals, axis=-1, keepdims=True)
        i1 = jnp.min(jnp.where(vals == m1, lane, float(n_experts)), axis=-1, keepdims=True)
        rest = jnp.where(lane == i1, neg, vals)
        m2 = jnp.max(rest, axis=-1, keepdims=True)
        i2 = jnp.min(jnp.where(rest == m2, lane, float(n_experts)), axis=-1, keepdims=True)
        return m1, i1, m2, i2

    best = None
    best_g = None
    for g in range(N_EXPERT_GROUPS):
        in_g = (lane >= g * per_group) & (lane < (g + 1) * per_group)
        m1, _, m2, _ = top2(jnp.where(in_g, sel, neg))
        val = m1 + m2
        if g == 0:
            best, best_g = val, jnp.zeros(val.shape, F32)
        else:
            upd = val > best
            best_g = jnp.where(upd, float(g), best_g)
            best = jnp.where(upd, val, best)
    lo = best_g * per_group
    in_best = (lane >= lo) & (lane < lo + per_group)
    _, e1, _, e2 = top2(jnp.where(in_best, sel, neg))
    w1 = jnp.sum(jnp.where(lane == e1, scores, 0.0), axis=-1, keepdims=True)
    w2 = jnp.sum(jnp.where(lane == e2, scores, 0.0), axis=-1, keepdims=True)
    tot = w1 + w2
    e1_ref[...] = e1.astype(jnp.int32)
    e2_ref[...] = e2.astype(jnp.int32)
    g1_ref[...] = w1 / tot
    g2_ref[...] = w2 / tot


def _router(xb, w_router, b_router, *, tm, name):
    m, d = xb.shape
    n_experts = w_router.shape[1]
    col = pl.BlockSpec((tm, 1), lambda i: (i, 0))
    return pl.pallas_call(
        functools.partial(_router_body, n_experts=n_experts), grid=(m // tm,),
        in_specs=[pl.BlockSpec((tm, d), lambda i: (i, 0)),
                  pl.BlockSpec((d, n_experts), lambda i: (0, 0)),
                  pl.BlockSpec((1, n_experts), lambda i: (0, 0))],
        out_specs=[col, col, col, col],
        out_shape=[jax.ShapeDtypeStruct((m, 1), jnp.int32), jax.ShapeDtypeStruct((m, 1), jnp.int32),
                   jax.ShapeDtypeStruct((m, 1), F32), jax.ShapeDtypeStruct((m, 1), F32)],
        compiler_params=_params(("parallel",)), name=name,
    )(xb, w_router, b_router.reshape(1, n_experts))


def _row_copy(src_hbm, row, dst, i, sem):
    return pltpu.make_async_copy(src_hbm.at[pl.ds(row, 1)], dst.at[pl.ds(i, 1)], sem)


def _gather_rows_body(tok_ref, x_hbm, o_ref, buf, sem, *, tm):
    base = pl.program_id(0) * tm

    def start(i, c):
        _row_copy(x_hbm, tok_ref[base + i], buf, i, sem).start()
        return c

    def wait(i, c):
        _row_copy(x_hbm, 0, buf, i, sem).wait()
        return c

    lax.fori_loop(0, tm, start, 0)
    lax.fori_loop(0, tm, wait, 0)
    o_ref[...] = buf[...].astype(o_ref.dtype)


def _gather_rows(x, row_token, *, tm, name):
    d = x.shape[1]
    n_rows = row_token.shape[0]
    grid_spec = pltpu.PrefetchScalarGridSpec(
        num_scalar_prefetch=1, grid=(n_rows // tm,),
        in_specs=[pl.BlockSpec(memory_space=pl.ANY)],
        out_specs=pl.BlockSpec((tm, d), lambda i, tok: (i, 0)),
        scratch_shapes=[pltpu.VMEM((tm, d), x.dtype), pltpu.SemaphoreType.DMA])
    return pl.pallas_call(
        functools.partial(_gather_rows_body, tm=tm), grid_spec=grid_spec,
        out_shape=jax.ShapeDtypeStruct((n_rows, d), BF16),
        compiler_params=_params(("arbitrary",)), name=name,
    )(row_token, x)


def _expert_up_body(tile_ref, exp_ref, wchunk_ref, ochunk_ref, new_ref, valid_ref,
                    x_ref, wg_ref, wu_ref, h_ref, wg_s, wu_s):
    s = pl.program_id(0)

    @pl.when(new_ref[s] == 1)
    def _():
        wg_s[...] = wg_ref[...].astype(BF16)
        wu_s[...] = wu_ref[...].astype(BF16)

    @pl.when(valid_ref[s] == 1)
    def _():
        x = x_ref[...]
        g = jnp.dot(x, wg_s[...], preferred_element_type=F32)
        u = jnp.dot(x, wu_s[...], preferred_element_type=F32)
        h_ref[...] = (jax.nn.silu(g) * u).astype(h_ref.dtype)

    @pl.when(valid_ref[s] == 0)
    def _():
        h_ref[...] = jnp.zeros(h_ref.shape, h_ref.dtype)


def _step_maps(layer):
    x_map = lambda s, tile, exp, wchunk, ochunk, new, valid: (tile[s], 0)
    w_map = lambda s, tile, exp, wchunk, ochunk, new, valid: (layer, exp[s], 0, wchunk[s])
    o_map = lambda s, tile, exp, wchunk, ochunk, new, valid: (tile[s], ochunk[s])
    return x_map, w_map, o_map


def _expert_up(steps, xs, w_gate, w_up, layer, *, tm, tn, name):
    n_rows, d = xs.shape
    d_exp = w_gate.shape[-1]
    x_map, w_map, o_map = _step_maps(layer)
    grid_spec = pltpu.PrefetchScalarGridSpec(
        num_scalar_prefetch=len(steps), grid=(steps[0].shape[0],),
        in_specs=[pl.BlockSpec((tm, d), x_map),
                  pl.BlockSpec((None, None, d, tn), w_map),
                  pl.BlockSpec((None, None, d, tn), w_map)],
        out_specs=pl.BlockSpec((tm, tn), o_map),
        scratch_shapes=[pltpu.VMEM((d, tn), BF16), pltpu.VMEM((d, tn), BF16)])
    return pl.pallas_call(
        _expert_up_body, grid_spec=grid_spec,
        out_shape=jax.ShapeDtypeStruct((n_rows, d_exp), BF16),
        compiler_params=_params(("arbitrary",)), name=name,
    )(*steps, xs, w_gate, w_up)


def _expert_down_body(tile_ref, exp_ref, wchunk_ref, ochunk_ref, new_ref, valid_ref,
                      h_ref, wd_ref, y_ref, wd_s):
    s = pl.program_id(0)

    @pl.when(new_ref[s] == 1)
    def _():
        wd_s[...] = wd_ref[...].astype(BF16)

    @pl.when(valid_ref[s] == 1)
    def _():
        y_ref[...] = jnp.dot(h_ref[...], wd_s[...], preferred_element_type=F32).astype(y_ref.dtype)

    @pl.when(valid_ref[s] == 0)
    def _():
        y_ref[...] = jnp.zeros(y_ref.shape, y_ref.dtype)


def _expert_down(steps, h, w_down, layer, *, tm, tn, name):
    n_rows, d_exp = h.shape
    d = w_down.shape[-1]
    x_map, w_map, o_map = _step_maps(layer)
    grid_spec = pltpu.PrefetchScalarGridSpec(
        num_scalar_prefetch=len(steps), grid=(steps[0].shape[0],),
        in_specs=[pl.BlockSpec((tm, d_exp), x_map),
                  pl.BlockSpec((None, None, d_exp, tn), w_map)],
        out_specs=pl.BlockSpec((tm, tn), o_map),
        scratch_shapes=[pltpu.VMEM((d_exp, tn), BF16)])
    return pl.pallas_call(
        _expert_down_body, grid_spec=grid_spec,
        out_shape=jax.ShapeDtypeStruct((n_rows, d), F32),
        compiler_params=_params(("arbitrary",)), name=name,
    )(*steps, h, w_down)


def _combine_body(p1_ref, p2_ref, y_hbm, x_ref, g1_ref, g2_ref, lg_ref, lb_ref, o_ref, ob_ref,
                  buf1, buf2, sem, *, tm, alpha):
    base = pl.program_id(0) * tm

    def start(i, c):
        _row_copy(y_hbm, p1_ref[base + i], buf1, i, sem).start()
        _row_copy(y_hbm, p2_ref[base + i], buf2, i, sem).start()
        return c

    def wait(i, c):
        _row_copy(y_hbm, 0, buf1, i, sem).wait()
        _row_copy(y_hbm, 0, buf2, i, sem).wait()
        return c

    lax.fori_loop(0, tm, start, 0)
    lax.fori_loop(0, tm, wait, 0)
    moe = g1_ref[...] * buf1[...] + g2_ref[...] * buf2[...]
    r = _ln(alpha * x_ref[...] + moe, lg_ref[...], lb_ref[...])
    o_ref[...] = r
    ob_ref[...] = r.astype(BF16)


def _combine(pos1, pos2, y, x, g1, g2, ln_g, ln_b, layer, alpha, *, tm, name):
    m, d = x.shape
    row = lambda i, p1, p2: (i, 0)
    grid_spec = pltpu.PrefetchScalarGridSpec(
        num_scalar_prefetch=2, grid=(m // tm,),
        in_specs=[pl.BlockSpec(memory_space=pl.ANY),
                  pl.BlockSpec((tm, d), row),
                  pl.BlockSpec((tm, 1), row),
                  pl.BlockSpec((tm, 1), row),
                  pl.BlockSpec((None, 1, d), lambda i, p1, p2: (layer, 0, 0)),
                  pl.BlockSpec((None, 1, d), lambda i, p1, p2: (layer, 0, 0))],
        out_specs=[pl.BlockSpec((tm, d), row), pl.BlockSpec((tm, d), row)],
        scratch_shapes=[pltpu.VMEM((tm, d), y.dtype), pltpu.VMEM((tm, d), y.dtype),
                        pltpu.SemaphoreType.DMA])
    return pl.pallas_call(
        functools.partial(_combine_body, tm=tm, alpha=alpha), grid_spec=grid_spec,
        out_shape=[jax.ShapeDtypeStruct((m, d), F32), jax.ShapeDtypeStruct((m, d), BF16)],
        compiler_params=_params(("arbitrary",)), name=name,
    )(pos1, pos2, y, x, g1, g2, ln_g, ln_b)


def _dispatch_plan(e1, e2, n_experts, tm, n_chunks_up, n_chunks_down):
    t = e1.shape[0]
    e_flat = jnp.concatenate([e1.reshape(t), e2.reshape(t)])
    onehot = (e_flat[:, None] == jnp.arange(n_experts, dtype=jnp.int32)[None, :]).astype(jnp.int32)
    csum = jnp.cumsum(onehot, axis=0)
    counts = csum[-1]
    rank = jnp.take_along_axis(csum, e_flat[:, None], axis=1)[:, 0] - 1
    tiles_e = (counts + tm - 1) // tm
    tile_end = jnp.cumsum(tiles_e)
    tile_start = tile_end - tiles_e
    n_tiles = tile_end[-1]
    n_tiles_max = (2 * t + tm - 1) // tm + n_experts
    n_rows_max = n_tiles_max * tm
    dest = tile_start[e_flat] * tm + rank
    token = jnp.concatenate([jnp.arange(t, dtype=jnp.int32)] * 2)
    row_token = jnp.zeros((n_rows_max,), jnp.int32).at[dest].set(token)

    def steps(n_chunks):
        n_steps = n_tiles_max * n_chunks
        n_valid = n_tiles * n_chunks
        s = jnp.arange(n_steps, dtype=jnp.int32)
        valid = s < n_valid
        sc = jnp.minimum(s, n_valid - 1)
        exp = jnp.searchsorted(tile_end * n_chunks, sc, side="right").astype(jnp.int32)
        exp = jnp.minimum(exp, n_experts - 1)
        local = sc - tile_start[exp] * n_chunks
        nt = jnp.maximum(tiles_e[exp], 1)
        wchunk = local // nt
        tile = tile_start[exp] + local % nt
        pad = s - n_valid
        tile = jnp.where(valid, tile, n_tiles + pad // n_chunks)
        ochunk = jnp.where(valid, wchunk, pad % n_chunks)
        key = exp * n_chunks + wchunk
        new = jnp.concatenate([jnp.ones((1,), jnp.bool_), key[1:] != key[:-1]])
        i32 = lambda a: a.astype(jnp.int32)
        return (i32(tile), i32(exp), i32(wchunk), i32(ochunk), i32(new & valid), i32(valid))

    return row_token, dest[:t], dest[t:], steps(n_chunks_up), steps(n_chunks_down)


def kernel(x_prompt, x_sample, cache_ckv, cache_krope, page_table, w_in, q_norm_g, kv_norm_g, w_qb, w_kvb, w_oa, sgu_ln_g, sgu_ln_b, w_s, b_s, w_ob, w_out, ln1_g, ln1_b, w_router, b_router, w_exp_gate, w_exp_up, w_exp_down, ln2_g, ln2_b):
    batch, seq, d_model = x_prompt.shape
    n_seq, dec_seq, _ = x_sample.shape
    assert dec_seq == 1
    depth = w_in.shape[0]
    q_rank = q_norm_g.shape[1]
    kv_rank = kv_norm_g.shape[1]
    rope_dim = cache_krope.shape[-1]
    n_heads = w_kvb.shape[2]
    nope_dim = w_qb.shape[2] // n_heads - rope_dim
    v_dim = w_kvb.shape[3] - nope_dim
    d_sgu = sgu_ln_g.shape[1]
    n_groups, chunk = w_s.shape[1], w_s.shape[2]
    n_experts = w_router.shape[1]
    d_exp = w_exp_gate.shape[3]
    n_pages = page_table.shape[1]
    past_len = n_pages * cache_ckv.shape[2]
    assert nope_dim == LANES and v_dim == LANES and 2 * rope_dim == LANES
    assert d_sgu == n_groups * LANES and chunk == LANES and seq % chunk == 0
    half = rope_dim // 2
    alpha = float((2 * depth) ** 0.25)
    scale = float((nope_dim + rope_dim) ** -0.5)

    n_prompt = batch * seq
    m = n_prompt + n_seq
    tm = _row_tile(m, ROW_TILE)

    c0 = q_rank + kv_rank
    c1 = c0 + rope_dim
    c2 = c1 + d_sgu
    c3 = c2 + d_sgu
    kr_w = w_in[:, :, c0:c1]
    w_small = jnp.concatenate(
        [w_in[:, :, :c1], -kr_w[:, :, half:], kr_w[:, :, :half]], axis=2).astype(BF16)
    w_u = w_in[:, :, c1:c2].astype(BF16)
    w_v = w_in[:, :, c2:c3].astype(BF16)
    w_g = w_in[:, :, c3:].astype(BF16)
    wq = w_qb.reshape(depth, q_rank, n_heads, nope_dim + rope_dim)
    wq_r = wq[..., nope_dim:]
    wq_ext = jnp.concatenate(
        [wq, -wq_r[..., half:], wq_r[..., :half]], axis=-1
    ).reshape(depth, q_rank, n_heads * 2 * LANES).astype(BF16)
    wk_flat = w_kvb[..., :nope_dim].reshape(depth, kv_rank, n_heads * LANES).astype(BF16)
    wv_flat = w_kvb[..., nope_dim:].reshape(depth, kv_rank, n_heads * LANES).astype(BF16)
    wk_t = jnp.transpose(w_kvb[..., :nope_dim], (0, 2, 3, 1)).astype(BF16)
    wv_h = jnp.transpose(w_kvb[..., nope_dim:], (0, 2, 1, 3)).astype(BF16)
    w_oa_b = w_oa.astype(BF16)
    w_ob_b = w_ob.astype(BF16)
    w_out_b = w_out.astype(BF16)
    w_s_first = jnp.repeat(w_s[:, :, 0, 0], LANES, axis=1).reshape(depth, 1, d_sgu)
    b_s_first = jnp.repeat(b_s[:, :, 0], LANES, axis=1).reshape(depth, 1, d_sgu)
    b_s_t = jnp.transpose(b_s, (0, 2, 1))
    g3 = lambda a: a.reshape(depth, 1, a.shape[1])
    q_norm3, kv_norm3 = g3(q_norm_g), g3(kv_norm_g)
    sgu_g3, sgu_b3 = g3(sgu_ln_g), g3(sgu_ln_b)
    ln1_g3, ln1_b3, ln2_g3, ln2_b3 = g3(ln1_g), g3(ln1_b), g3(ln2_g), g3(ln2_b)

    pos = jnp.concatenate([jnp.tile(jnp.arange(seq, dtype=jnp.int32), batch),
                           jnp.full((n_seq,), past_len, jnp.int32)])
    inv = ROPE_THETA ** (-jnp.arange(half, dtype=F32) / half)
    ang = pos.astype(F32)[:, None] * inv[None, :]
    cs = jnp.concatenate([jnp.cos(ang), jnp.cos(ang), jnp.sin(ang), jnp.sin(ang)], axis=1)

    x = jnp.concatenate([x_prompt.reshape(n_prompt, d_model), x_sample.reshape(n_seq, d_model)])
    xb = x.astype(BF16)

    n_small = w_small.shape[-1]
    tn_q = _col_tile(wq_ext.shape[-1], COL_TILE)
    heads_q = tn_q // (2 * LANES)
    tn_d = _col_tile(d_model, COL_TILE)
    up_chunks = d_exp // _col_tile(d_exp, EXPERT_COL_TILE)
    down_chunks = d_model // _col_tile(d_model, DOWN_COL_TILE)
    cs_spec = (cs, (tm, LANES), lambda i, j: (i, 0))

    ckv_p, kr_p, ckv_d, kr_d, v_d = [], [], [], [], []
    for l in range(depth):
        qn, ckv, kr, kr_pad = _matmul(
            xb, w_small, l, tm=tm, tn=n_small,
            epi=functools.partial(_epi_small, q_rank=q_rank, kv_rank=kv_rank),
            extras=[cs_spec,
                    (q_norm3, (None, 1, q_rank), lambda i, j: (l, 0, 0)),
                    (kv_norm3, (None, 1, kv_rank), lambda i, j: (l, 0, 0))],
            outs=[(q_rank, q_rank, BF16), (kv_rank, kv_rank, F32), (rope_dim, rope_dim, F32),
                  (LANES, LANES, BF16)],
            name=f"in_small_{l}")
        (u,) = _matmul(xb, w_u, l, tm=tm, tn=_col_tile(d_sgu, COL_TILE), epi=_epi_gelu,
                       outs=[(d_sgu, _col_tile(d_sgu, COL_TILE), BF16)], name=f"in_u_{l}")
        (v_pre,) = _matmul(xb, w_v, l, tm=tm, tn=_col_tile(d_sgu, COL_TILE), epi=_epi_gelu,
                           outs=[(d_sgu, _col_tile(d_sgu, COL_TILE), F32)], name=f"in_v_{l}")
        v = _ln_rows(v_pre, sgu_g3, sgu_b3, l, tm=_row_tile(m, 512), name=f"sgu_ln_{l}")
        (gates,) = _matmul(xb, w_g, l, tm=tm, tn=tn_d, epi=_epi_sigmoid,
                           outs=[(2 * d_model, tn_d, BF16)], name=f"in_gates_{l}")

        (q_full,) = _matmul(qn, wq_ext, l, tm=tm, tn=tn_q,
                            epi=functools.partial(_epi_q, heads=heads_q, scale=scale),
                            extras=[cs_spec],
                            outs=[(n_heads * 2 * LANES, tn_q, BF16)], name=f"q_proj_{l}")
        k_full, v_full = _kv_expand(ckv, kr_pad, wk_flat, wv_flat, l, n_prompt,
                                    tm=_row_tile(n_prompt, 1024),
                                    heads_per_step=min(n_heads, 8), name=f"kv_expand_{l}")
        o_prompt = _prompt_attn(q_full, k_full, v_full, batch, seq,
                                tq=min(seq, ATTN_Q_TILE), name=f"prompt_attn_{l}")
        q_dec = q_full[n_prompt:]
        q_lat = _headwise(q_dec, wk_t, l, in_stride=2 * LANES,
                          heads_per_step=min(n_heads, 8), name=f"q_absorb_{l}")
        q_rope_dec = q_dec.reshape(n_seq, n_heads, 2 * LANES)[:, :, LANES:]
        o_lat = _paged_attn(page_table, q_lat.reshape(n_seq, n_heads, kv_rank), q_rope_dec,
                            ckv[n_prompt:].reshape(n_seq, 1, kv_rank),
                            kr[n_prompt:].reshape(n_seq, 1, rope_dim),
                            cache_ckv, cache_krope, l, name=f"paged_attn_{l}")
        o_dec = _headwise(o_lat.reshape(n_seq, n_heads * kv_rank), wv_h, l, in_stride=kv_rank,
                          heads_per_step=min(n_heads, 8), name=f"v_up_{l}")
        o = jnp.concatenate([o_prompt, o_dec])

        ub_prompt = _sgu_prompt(u, v, w_s, b_s_t, l, n_prompt, name=f"sgu_prompt_{l}")
        ub_dec = _sgu_sample(u[n_prompt:], v[n_prompt:], w_s_first, b_s_first, l,
                             name=f"sgu_sample_{l}")
        ub = jnp.concatenate([ub_prompt, ub_dec])

        h = _gate_merge(o, ub, w_oa_b, w_ob_b, gates, l, tm=tm, tn=_col_tile(d_model, 512),
                        name=f"gate_merge_{l}")
        (mix,) = _matmul(h, w_out_b, l, tm=tm, tn=tn_d, epi=_epi_id,
                         outs=[(d_model, tn_d, F32)], name=f"out_proj_{l}")
        x1, x1b = _ln_residual(x, mix, ln1_g3, ln1_b3, l, alpha, tm=_row_tile(m, 128),
                               name=f"ln1_{l}")

        e1, e2, g1, g2 = _router(x1b, w_router, b_router, tm=tm, name=f"router_{l}")
        row_token, pos1, pos2, steps_up, steps_down = _dispatch_plan(
            e1, e2, n_experts, EXPERT_ROW_TILE, up_chunks, down_chunks)
        xs = _gather_rows(x1, row_token, tm=EXPERT_ROW_TILE, name=f"moe_gather_{l}")
        hid = _expert_up(steps_up, xs, w_exp_gate, w_exp_up, l, tm=EXPERT_ROW_TILE,
                         tn=d_exp // up_chunks, name=f"expert_up_{l}")
        y = _expert_down(steps_down, hid, w_exp_down, l, tm=EXPERT_ROW_TILE,
                         tn=d_model // down_chunks, name=f"expert_down_{l}")
        x, xb = _combine(pos1, pos2, y, x1, g1, g2, ln2_g3, ln2_b3, l, alpha,
                         tm=_row_tile(m, 128), name=f"moe_combine_{l}")

        ckv_p.append(ckv[:n_prompt].reshape(batch, seq, kv_rank))
        kr_p.append(kr[:n_prompt].reshape(batch, seq, rope_dim))
        ckv_d.append(ckv[n_prompt:].reshape(n_seq, 1, kv_rank))
        kr_d.append(kr[n_prompt:].reshape(n_seq, 1, rope_dim))
        v_d.append(v[n_prompt:].reshape(n_seq, 1, d_sgu))

    return (x[:n_prompt].reshape(batch, seq, d_model), x[n_prompt:].reshape(n_seq, 1, d_model),
            jnp.stack(ckv_p), jnp.stack(kr_p), jnp.stack(ckv_d), jnp.stack(kr_d), jnp.stack(v_d))
```

```python
import functools

import jax
import jax.numpy as jnp
from jax import lax
from jax.experimental import pallas as pl
from jax.experimental.pallas import tpu as pltpu

F32 = jnp.float32
BF16 = jnp.bfloat16

ROPE_THETA = 10000.0
LN_EPS = 1e-5
RMS_EPS = 1e-6
N_EXPERT_GROUPS = 4

LANES = 128
SUBLANES = 8
VMEM_LIMIT_BYTES = 52 * 1024 * 1024

ROW_TILE = 640
COL_TILE = 1024
EXPERT_ROW_TILE = 128
EXPERT_COL_TILE = 512
DOWN_COL_TILE = 2048
ATTN_Q_TILE = 512
GATHER_UNROLL = 8


def _params(semantics):
    return pltpu.CompilerParams(dimension_semantics=semantics, vmem_limit_bytes=VMEM_LIMIT_BYTES)


def _row_tile(m, target):
    best = None
    for d in range(SUBLANES, min(m, target) + 1, SUBLANES):
        if m % d == 0:
            best = d
    return best if best is not None else m


def _col_tile(n, target):
    best = None
    for d in range(LANES, min(n, target) + 1, LANES):
        if n % d == 0:
            best = d
    return best if best is not None else n


def _mm_body(*refs, n_extra, n_out, epi, w_transposed):
    x_ref, w_ref = refs[0], refs[1]
    extra = refs[2:2 + n_extra]
    outs = refs[2 + n_extra:2 + n_extra + n_out]
    if w_transposed:
        acc = lax.dot_general(x_ref[...], w_ref[...], (((1,), (1,)), ((), ())),
                              preferred_element_type=F32)
    else:
        acc = jnp.dot(x_ref[...], w_ref[...], preferred_element_type=F32)
    res = epi(acc, *[e[...] for e in extra])
    if not isinstance(res, tuple):
        res = (res,)
    for o_ref, r in zip(outs, res):
        o_ref[...] = r.astype(o_ref.dtype)


def _matmul(x, w, layer, *, tm, tn, epi, extras=(), outs, name, w_transposed=False):
    m, k = x.shape
    n = w.shape[1] if w_transposed else w.shape[2]
    grid = (m // tm, n // tn)
    if w_transposed:
        w_spec = pl.BlockSpec((None, tn, k), lambda i, j: (layer, j, 0))
    else:
        w_spec = pl.BlockSpec((None, k, tn), lambda i, j: (layer, 0, j))
    in_specs = [pl.BlockSpec((tm, k), lambda i, j: (i, 0)), w_spec]
    in_specs += [pl.BlockSpec(bs, im) for (_, bs, im) in extras]
    out_specs = [pl.BlockSpec((tm, ct), lambda i, j: (i, j)) for (_, ct, _) in outs]
    out_shape = [jax.ShapeDtypeStruct((m, c), dt) for (c, _, dt) in outs]
    body = functools.partial(_mm_body, n_extra=len(extras), n_out=len(outs), epi=epi,
                             w_transposed=w_transposed)
    res = pl.pallas_call(
        body, grid=grid, in_specs=in_specs, out_specs=out_specs, out_shape=out_shape,
        compiler_params=_params(("parallel", "arbitrary")), name=name,
    )(x, w, *[a for (a, _, _) in extras])
    return res


def _rope_pair(t, cs):
    tt = t * cs
    return tt + pltpu.roll(tt, LANES // 2, 1)


def _epi_small(acc, cs, gq, gkv, *, q_rank, kv_rank):
    qa = acc[:, :q_rank]
    ca = acc[:, q_rank:q_rank + kv_rank]
    t = acc[:, q_rank + kv_rank:]
    qn = qa * lax.rsqrt(jnp.mean(qa * qa, axis=-1, keepdims=True) + RMS_EPS) * gq
    ckv = ca * lax.rsqrt(jnp.mean(ca * ca, axis=-1, keepdims=True) + RMS_EPS) * gkv
    r = _rope_pair(t, cs)
    lane = lax.broadcasted_iota(jnp.int32, r.shape, 1)
    kr_pad = jnp.where(lane < LANES // 2, r, 0.0)
    return qn, ckv, r[:, :LANES // 2], kr_pad


def _epi_q(acc, cs, *, heads, scale):
    parts = []
    for h in range(heads):
        blk = acc[:, h * 2 * LANES:(h + 1) * 2 * LANES]
        parts.append(blk[:, :LANES] * scale)
        parts.append(_rope_pair(blk[:, LANES:], cs) * scale)
    return jnp.concatenate(parts, axis=1)


def _epi_gelu(acc):
    return jax.nn.gelu(acc)


def _epi_sigmoid(acc):
    return jax.nn.sigmoid(acc)


def _epi_id(acc):
    return acc


def _ln(x, g, b):
    mu = jnp.mean(x, axis=-1, keepdims=True)
    xc = x - mu
    var = jnp.mean(xc * xc, axis=-1, keepdims=True)
    return xc * lax.rsqrt(var + LN_EPS) * g + b


def _ln_rows_body(x_ref, g_ref, b_ref, o_ref):
    o_ref[...] = _ln(x_ref[...], g_ref[...], b_ref[...])


def _ln_rows(x, g, b, layer, *, tm, name):
    m, n = x.shape
    return pl.pallas_call(
        _ln_rows_body, grid=(m // tm,),
        in_specs=[pl.BlockSpec((tm, n), lambda i: (i, 0)),
                  pl.BlockSpec((None, 1, n), lambda i: (layer, 0, 0)),
                  pl.BlockSpec((None, 1, n), lambda i: (layer, 0, 0))],
        out_specs=pl.BlockSpec((tm, n), lambda i: (i, 0)),
        out_shape=jax.ShapeDtypeStruct((m, n), F32),
        compiler_params=_params(("parallel",)), name=name,
    )(x, g, b)


def _ln_residual_body(x_ref, y_ref, g_ref, b_ref, o_ref, ob_ref, *, alpha):
    r = _ln(alpha * x_ref[...] + y_ref[...], g_ref[...], b_ref[...])
    o_ref[...] = r
    ob_ref[...] = r.astype(BF16)


def _ln_residual(x, y, g, b, layer, alpha, *, tm, name):
    m, n = x.shape
    return pl.pallas_call(
        functools.partial(_ln_residual_body, alpha=alpha), grid=(m // tm,),
        in_specs=[pl.BlockSpec((tm, n), lambda i: (i, 0)),
                  pl.BlockSpec((tm, n), lambda i: (i, 0)),
                  pl.BlockSpec((None, 1, n), lambda i: (layer, 0, 0)),
                  pl.BlockSpec((None, 1, n), lambda i: (layer, 0, 0))],
        out_specs=[pl.BlockSpec((tm, n), lambda i: (i, 0)),
                   pl.BlockSpec((tm, n), lambda i: (i, 0))],
        out_shape=[jax.ShapeDtypeStruct((m, n), F32), jax.ShapeDtypeStruct((m, n), BF16)],
        compiler_params=_params(("parallel",)), name=name,
    )(x, y, g, b)


def _kv_expand_body(c_ref, kr_ref, wk_ref, wv_ref, k_ref, v_ref, *, heads):
    c = c_ref[...].astype(BF16)
    kn = jnp.dot(c, wk_ref[...], preferred_element_type=F32).astype(BF16)
    kr = kr_ref[...]
    for h in range(heads):
        k_ref[:, h * 2 * LANES:h * 2 * LANES + LANES] = kn[:, h * LANES:(h + 1) * LANES]
        k_ref[:, h * 2 * LANES + LANES:(h + 1) * 2 * LANES] = kr
    v_ref[...] = jnp.dot(c, wv_ref[...], preferred_element_type=F32).astype(BF16)


def _kv_expand(ckv, kr_pad, wk, wv, layer, n_rows, *, tm, heads_per_step, name):
    kv_rank = ckv.shape[1]
    n_heads = wk.shape[-1] // LANES
    hb = heads_per_step
    return pl.pallas_call(
        functools.partial(_kv_expand_body, heads=hb), grid=(n_rows // tm, n_heads // hb),
        in_specs=[pl.BlockSpec((tm, kv_rank), lambda i, j: (i, 0)),
                  pl.BlockSpec((tm, LANES), lambda i, j: (i, 0)),
                  pl.BlockSpec((None, kv_rank, hb * LANES), lambda i, j: (layer, 0, j)),
                  pl.BlockSpec((None, kv_rank, hb * LANES), lambda i, j: (layer, 0, j))],
        out_specs=[pl.BlockSpec((tm, hb * 2 * LANES), lambda i, j: (i, j)),
                   pl.BlockSpec((tm, hb * LANES), lambda i, j: (i, j))],
        out_shape=[jax.ShapeDtypeStruct((n_rows, n_heads * 2 * LANES), BF16),
                   jax.ShapeDtypeStruct((n_rows, n_heads * LANES), BF16)],
        compiler_params=_params(("parallel", "arbitrary")), name=name,
    )(ckv, kr_pad, wk, wv)


def _prompt_attn_body(q_ref, k_ref, v_ref, o_ref, *, tq, seq):
    nt = (((1,), (1,)), ((), ()))
    row = lax.broadcasted_iota(jnp.int32, (tq, tq), 0)
    col = lax.broadcasted_iota(jnp.int32, (tq, tq), 1)
    causal = col <= row
    for qi in range(seq // tq):
        off, kend = qi * tq, (qi + 1) * tq
        q = q_ref[off:kend, :]
        sd = lax.dot_general(q, k_ref[off:kend, :], nt, preferred_element_type=F32)
        sd = jnp.where(causal, sd, -jnp.inf)
        m = jnp.max(sd, axis=-1, keepdims=True)
        if qi > 0:
            so = lax.dot_general(q, k_ref[0:off, :], nt, preferred_element_type=F32)
            m = jnp.maximum(m, jnp.max(so, axis=-1, keepdims=True))
            po = jnp.exp(so - m)
        pd = jnp.exp(sd - m)
        l = jnp.sum(pd, axis=-1, keepdims=True)
        o = jnp.dot(pd.astype(BF16), v_ref[off:kend, :], preferred_element_type=F32)
        if qi > 0:
            l = l + jnp.sum(po, axis=-1, keepdims=True)
            o = o + jnp.dot(po.astype(BF16), v_ref[0:off, :], preferred_element_type=F32)
        o_ref[off:kend, :] = (o / l).astype(o_ref.dtype)


def _prompt_attn(q, k, v, batch, seq, *, tq, name):
    n_heads = v.shape[1] // LANES
    return pl.pallas_call(
        functools.partial(_prompt_attn_body, tq=tq, seq=seq), grid=(batch, n_heads),
        in_specs=[pl.BlockSpec((seq, 2 * LANES), lambda b, h: (b, h)),
                  pl.BlockSpec((seq, 2 * LANES), lambda b, h: (b, h)),
                  pl.BlockSpec((seq, LANES), lambda b, h: (b, h))],
        out_specs=pl.BlockSpec((seq, LANES), lambda b, h: (b, h)),
        out_shape=jax.ShapeDtypeStruct((batch * seq, n_heads * LANES), BF16),
        compiler_params=_params(("parallel", "parallel")), name=name,
    )(q, k, v)


def _headwise_body(x_ref, w_ref, o_ref, *, heads, in_stride, d_in, d_out):
    for h in range(heads):
        xh = x_ref[:, h * in_stride:h * in_stride + d_in]
        o_ref[:, h * d_out:(h + 1) * d_out] = jnp.dot(
            xh, w_ref[h], preferred_element_type=F32).astype(o_ref.dtype)


def _headwise(x, w, layer, *, in_stride, heads_per_step, name):
    m = x.shape[0]
    n_heads, d_in, d_out = w.shape[1:]
    hb = heads_per_step
    return pl.pallas_call(
        functools.partial(_headwise_body, heads=hb, in_stride=in_stride, d_in=d_in, d_out=d_out),
        grid=(n_heads // hb,),
        in_specs=[pl.BlockSpec((m, hb * in_stride), lambda j: (0, j)),
                  pl.BlockSpec((None, hb, d_in, d_out), lambda j: (layer, j, 0, 0))],
        out_specs=pl.BlockSpec((m, hb * d_out), lambda j: (0, j)),
        out_shape=jax.ShapeDtypeStruct((m, n_heads * d_out), BF16),
        compiler_params=_params(("parallel",)), name=name,
    )(x, w)


def _paged_attn_body(pt_ref, ql_ref, qr_ref, cn_ref, rn_ref, c_hbm, r_hbm, o_ref,
                     cbuf, rbuf, sem, *, n_pages, rope_dim, layer):
    b = pl.program_id(0)
    slot = b % 2

    def page_copies(seq, i, s):
        pg = pt_ref[seq, i]
        return (pltpu.make_async_copy(c_hbm.at[layer, pg], cbuf.at[s, i], sem.at[s]),
                pltpu.make_async_copy(r_hbm.at[layer, pg], rbuf.at[s, i], sem.at[s]))

    def issue(seq, s):
        def body(i, c):
            for cp in page_copies(seq, i, s):
                cp.start()
            return c

        lax.fori_loop(0, n_pages, body, 0, unroll=GATHER_UNROLL)

    @pl.when(b == 0)
    def _():
        issue(0, 0)

    def wait(i, c):
        for cp in page_copies(b, i, slot):
            cp.wait()
        return c

    lax.fori_loop(0, n_pages, wait, 0, unroll=GATHER_UNROLL)

    @pl.when(b + 1 < pl.num_programs(0))
    def _():
        issue(b + 1, 1 - slot)

    ql = ql_ref[...]
    qr = qr_ref[...][:, :rope_dim]
    nt = (((1,), (1,)), ((), ()))
    ss = []
    for i in range(n_pages):
        c = cbuf[slot, i].astype(BF16)
        r_t = rbuf[slot, i].astype(BF16)
        ss.append(lax.dot_general(ql, c, nt, preferred_element_type=F32)
                  + jnp.dot(qr, r_t, preferred_element_type=F32))
    s = jnp.concatenate(ss, axis=1)
    cn = cn_ref[...]
    rn = rn_ref[...]
    s_self = (jnp.sum(ql.astype(F32) * cn, axis=-1, keepdims=True)
              + jnp.sum(qr.astype(F32) * rn, axis=-1, keepdims=True))
    m = jnp.maximum(jnp.max(s, axis=-1, keepdims=True), s_self)
    p = jnp.exp(s - m)
    p_self = jnp.exp(s_self - m)
    l = jnp.sum(p, axis=-1, keepdims=True) + p_self
    pv = p_self * cn
    for i in range(n_pages):
        pv = pv + jnp.dot(p[:, i * LANES:(i + 1) * LANES].astype(BF16),
                          cbuf[slot, i].astype(BF16), preferred_element_type=F32)
    o_ref[...] = (pv / l).astype(o_ref.dtype)


def _paged_attn(page_table, q_lat, q_rope, c_new, r_new, cache_c, cache_r_t, layer, *, name):
    n_seq, n_heads, kv_rank = q_lat.shape
    n_pages = page_table.shape[1]
    page = cache_c.shape[2]
    rope_dim = cache_r_t.shape[2]
    assert page == LANES
    seq_map = lambda b, pt: (b, 0, 0)
    grid_spec = pltpu.PrefetchScalarGridSpec(
        num_scalar_prefetch=1, grid=(n_seq,),
        in_specs=[pl.BlockSpec((None, n_heads, kv_rank), seq_map),
                  pl.BlockSpec((None, n_heads, LANES), seq_map),
                  pl.BlockSpec((None, 1, kv_rank), seq_map),
                  pl.BlockSpec((None, 1, rope_dim), seq_map),
                  pl.BlockSpec(memory_space=pl.ANY),
                  pl.BlockSpec(memory_space=pl.ANY)],
        out_specs=pl.BlockSpec((None, n_heads, kv_rank), seq_map),
        scratch_shapes=[pltpu.VMEM((2, n_pages, page, kv_rank), cache_c.dtype),
                        pltpu.VMEM((2, n_pages, rope_dim, page), cache_r_t.dtype),
                        pltpu.SemaphoreType.DMA((2,))])
    return pl.pallas_call(
        functools.partial(_paged_attn_body, n_pages=n_pages, rope_dim=rope_dim, layer=layer),
        grid_spec=grid_spec,
        out_shape=jax.ShapeDtypeStruct((n_seq, n_heads, kv_rank), BF16),
        compiler_params=_params(("arbitrary",)), name=name,
    )(page_table, q_lat, q_rope, c_new, r_new, cache_c, cache_r_t)


def _sgu_prompt_body(u_ref, v_ref, w_ref, bt_ref, o_ref, *, groups):
    chunk = w_ref.shape[-1]
    row = lax.broadcasted_iota(jnp.int32, (chunk, chunk), 0)
    col = lax.broadcasted_iota(jnp.int32, (chunk, chunk), 1)
    tril = col <= row
    for g in range(groups):
        w = jnp.where(tril, w_ref[g], 0.0).astype(BF16)
        vg = v_ref[:, g * LANES:(g + 1) * LANES].astype(BF16)
        mix = jnp.dot(w, vg, preferred_element_type=F32) + bt_ref[:, g:g + 1]
        ug = u_ref[:, g * LANES:(g + 1) * LANES].astype(F32)
        o_ref[:, g * LANES:(g + 1) * LANES] = (ug * mix).astype(o_ref.dtype)


def _sgu_prompt(u, v, w_s, b_s_t, layer, n_rows, *, name):
    groups, chunk = w_s.shape[1], w_s.shape[2]
    d_sgu = u.shape[1]
    return pl.pallas_call(
        functools.partial(_sgu_prompt_body, groups=groups), grid=(n_rows // chunk,),
        in_specs=[pl.BlockSpec((chunk, d_sgu), lambda i: (i, 0)),
                  pl.BlockSpec((chunk, d_sgu), lambda i: (i, 0)),
                  pl.BlockSpec((None, groups, chunk, chunk), lambda i: (layer, 0, 0, 0)),
                  pl.BlockSpec((None, chunk, groups), lambda i: (layer, 0, 0))],
        out_specs=pl.BlockSpec((chunk, d_sgu), lambda i: (i, 0)),
        out_shape=jax.ShapeDtypeStruct((n_rows, d_sgu), BF16),
        compiler_params=_params(("parallel",)), name=name,
    )(u, v, w_s, b_s_t)


def _sgu_sample_body(u_ref, v_ref, w_ref, b_ref, o_ref):
    mix = v_ref[...] * w_ref[...] + b_ref[...]
    o_ref[...] = (u_ref[...].astype(F32) * mix).astype(o_ref.dtype)


def _sgu_sample(u, v, w_row, b_row, layer, *, name):
    m, d_sgu = u.shape
    return pl.pallas_call(
        _sgu_sample_body, grid=(1,),
        in_specs=[pl.BlockSpec((m, d_sgu), lambda i: (0, 0)),
                  pl.BlockSpec((m, d_sgu), lambda i: (0, 0)),
                  pl.BlockSpec((None, 1, d_sgu), lambda i: (layer, 0, 0)),
                  pl.BlockSpec((None, 1, d_sgu), lambda i: (layer, 0, 0))],
        out_specs=pl.BlockSpec((m, d_sgu), lambda i: (0, 0)),
        out_shape=jax.ShapeDtypeStruct((m, d_sgu), BF16),
        compiler_params=_params(("arbitrary",)), name=name,
    )(u, v, w_row, b_row)


def _gate_merge_body(o_ref, ub_ref, wa_ref, wb_ref, ga_ref, gb_ref, h_ref):
    ya = jnp.dot(o_ref[...], wa_ref[...], preferred_element_type=F32)
    yb = jnp.dot(ub_ref[...], wb_ref[...], preferred_element_type=F32)
    h_ref[...] = (ga_ref[...].astype(F32) * ya + gb_ref[...].astype(F32) * yb).astype(h_ref.dtype)


def _gate_merge(o, ub, w_oa, w_ob, gates, layer, *, tm, tn, name):
    m, ka = o.shape
    kb = ub.shape[1]
    n = w_oa.shape[-1]
    nj = n // tn
    return pl.pallas_call(
        _gate_merge_body, grid=(m // tm, nj),
        in_specs=[pl.BlockSpec((tm, ka), lambda i, j: (i, 0)),
                  pl.BlockSpec((tm, kb), lambda i, j: (i, 0)),
                  pl.BlockSpec((None, ka, tn), lambda i, j: (layer, 0, j)),
                  pl.BlockSpec((None, kb, tn), lambda i, j: (layer, 0, j)),
                  pl.BlockSpec((tm, tn), lambda i, j: (i, j)),
                  pl.BlockSpec((tm, tn), lambda i, j: (i, j + nj))],
        out_specs=pl.BlockSpec((tm, tn), lambda i, j: (i, j)),
        out_shape=jax.ShapeDtypeStruct((m, n), BF16),
        compiler_params=_params(("parallel", "arbitrary")), name=name,
    )(o, ub, w_oa, w_ob, gates, gates)


def _router_body(x_ref, w_ref, b_ref, e1_ref, e2_ref, g1_ref, g2_ref, *, n_experts):
    logits = jnp.dot(x_ref[...], w_ref[...].astype(BF16), preferred_element_type=F32)
    scores = jax.nn.sigmoid(logits)
    sel = scores + b_ref[...]
    per_group = n_experts // N_EXPERT_GROUPS
    lane = lax.broadcasted_iota(jnp.int32, sel.shape, 1).astype(F32)
    neg = -jnp.inf

    def top2(vals):
        m1 = jnp.max(vals, axis=-1, keepdims=True)
        i1 = jnp.min(jnp.where(vals == m1, lane, float(n_experts)), axis=-1, keepdims=True)
        rest = jnp.where(lane == i1, neg, vals)
        m2 = jnp.max(rest, axis=-1, keepdims=True)
        i2 = jnp.min(jnp.where(rest == m2, lane, float(n_experts)), axis=-1, keepdims=True)
        return m1, i1, m2, i2

    best = None
    best_g = None
    for g in range(N_EXPERT_GROUPS):
        in_g = (lane >= g * per_group) & (lane < (g + 1) * per_group)
        m1, _, m2, _ = top2(jnp.where(in_g, sel, neg))
        val = m1 + m2
        if g == 0:
            best, best_g = val, jnp.zeros(val.shape, F32)
        else:
            upd = val > best
            best_g = jnp.where(upd, float(g), best_g)
            best = jnp.where(upd, val, best)
    lo = best_g * per_group
    in_best = (lane >= lo) & (lane < lo + per_group)
    _, e1, _, e2 = top2(jnp.where(in_best, sel, neg))
    w1 = jnp.sum(jnp.where(lane == e1, scores, 0.0), axis=-1, keepdims=True)
    w2 = jnp.sum(jnp.where(lane == e2, scores, 0.0), axis=-1, keepdims=True)
    tot = w1 + w2
    e1_ref[...] = e1.astype(jnp.int32)
    e2_ref[...] = e2.astype(jnp.int32)
    g1_ref[...] = w1 / tot
    g2_ref[...] = w2 / tot


def _router(xb, w_router, b_router, *, tm, name):
    m, d = xb.shape
    n_experts = w_router.shape[1]
    col = pl.BlockSpec((tm, 1), lambda i: (i, 0))
    return pl.pallas_call(
        functools.partial(_router_body, n_experts=n_experts), grid=(m // tm,),
        in_specs=[pl.BlockSpec((tm, d), lambda i: (i, 0)),
                  pl.BlockSpec((d, n_experts), lambda i: (0, 0)),
                  pl.BlockSpec((1, n_experts), lambda i: (0, 0))],
        out_specs=[col, col, col, col],
        out_shape=[jax.ShapeDtypeStruct((m, 1), jnp.int32), jax.ShapeDtypeStruct((m, 1), jnp.int32),
                   jax.ShapeDtypeStruct((m, 1), F32), jax.ShapeDtypeStruct((m, 1), F32)],
        compiler_params=_params(("parallel",)), name=name,
    )(xb, w_router, b_router.reshape(1, n_experts))


def _row_copy(src_hbm, row, dst, i, sem):
    return pltpu.make_async_copy(src_hbm.at[pl.ds(row, 1)], dst.at[pl.ds(i, 1)], sem)


def _gather_rows_body(tok_ref, x_hbm, o_hbm, sem, *, tm):
    step = pl.program_id(0)
    base = step * tm

    def start(i, c):
        pltpu.make_async_copy(x_hbm.at[pl.ds(tok_ref[base + i], 1)],
                              o_hbm.at[pl.ds(base + i, 1)], sem).start()
        return c

    def wait(i, c):
        pltpu.make_async_copy(x_hbm.at[pl.ds(0, 1)], o_hbm.at[pl.ds(0, 1)], sem).wait()
        return c

    lax.fori_loop(0, tm, start, 0, unroll=GATHER_UNROLL)

    @pl.when(step > 0)
    def _():
        lax.fori_loop(0, tm, wait, 0, unroll=GATHER_UNROLL)

    @pl.when(step == pl.num_programs(0) - 1)
    def _():
        lax.fori_loop(0, tm, wait, 0, unroll=GATHER_UNROLL)


def _gather_rows(x, row_token, *, tm, name):
    d = x.shape[1]
    n_rows = row_token.shape[0]
    grid_spec = pltpu.PrefetchScalarGridSpec(
        num_scalar_prefetch=1, grid=(n_rows // tm,),
        in_specs=[pl.BlockSpec(memory_space=pl.ANY)],
        out_specs=pl.BlockSpec(memory_space=pl.ANY),
        scratch_shapes=[pltpu.SemaphoreType.DMA])
    return pl.pallas_call(
        functools.partial(_gather_rows_body, tm=tm), grid_spec=grid_spec,
        out_shape=jax.ShapeDtypeStruct((n_rows, d), x.dtype),
        compiler_params=_params(("arbitrary",)), name=name,
    )(row_token, x)


def _expert_up_body(tile_ref, exp_ref, wchunk_ref, ochunk_ref, valid_ref,
                    x_ref, wg_ref, wu_ref, h_ref):
    s = pl.program_id(0)

    @pl.when(valid_ref[s] == 1)
    def _():
        x = x_ref[...]
        g = jnp.dot(x, wg_ref[...], preferred_element_type=F32)
        u = jnp.dot(x, wu_ref[...], preferred_element_type=F32)
        h_ref[...] = (jax.nn.silu(g) * u).astype(h_ref.dtype)

    @pl.when(valid_ref[s] == 0)
    def _():
        h_ref[...] = jnp.zeros(h_ref.shape, h_ref.dtype)


def _step_maps(layer):
    x_map = lambda s, tile, exp, wchunk, ochunk, valid: (tile[s], 0)
    w_map = lambda s, tile, exp, wchunk, ochunk, valid: (layer, exp[s], 0, wchunk[s])
    o_map = lambda s, tile, exp, wchunk, ochunk, valid: (tile[s], ochunk[s])
    return x_map, w_map, o_map


def _expert_up(steps, xs, w_gate, w_up, layer, *, tm, tn, name):
    n_rows, d = xs.shape
    d_exp = w_gate.shape[-1]
    x_map, w_map, o_map = _step_maps(layer)
    grid_spec = pltpu.PrefetchScalarGridSpec(
        num_scalar_prefetch=len(steps), grid=(steps[0].shape[0],),
        in_specs=[pl.BlockSpec((tm, d), x_map),
                  pl.BlockSpec((None, None, d, tn), w_map),
                  pl.BlockSpec((None, None, d, tn), w_map)],
        out_specs=pl.BlockSpec((tm, tn), o_map))
    return pl.pallas_call(
        _expert_up_body, grid_spec=grid_spec,
        out_shape=jax.ShapeDtypeStruct((n_rows, d_exp), BF16),
        compiler_params=_params(("arbitrary",)), name=name,
    )(*steps, xs, w_gate, w_up)


def _expert_down_body(tile_ref, exp_ref, wchunk_ref, ochunk_ref, valid_ref, h_ref, wd_ref, y_ref):
    s = pl.program_id(0)

    @pl.when(valid_ref[s] == 1)
    def _():
        y_ref[...] = jnp.dot(h_ref[...].astype(F32), wd_ref[...],
                             preferred_element_type=F32).astype(y_ref.dtype)

    @pl.when(valid_ref[s] == 0)
    def _():
        y_ref[...] = jnp.zeros(y_ref.shape, y_ref.dtype)


def _expert_down(steps, h, w_down, layer, *, tm, tn, name):
    n_rows, d_exp = h.shape
    d = w_down.shape[-1]
    x_map, w_map, o_map = _step_maps(layer)
    grid_spec = pltpu.PrefetchScalarGridSpec(
        num_scalar_prefetch=len(steps), grid=(steps[0].shape[0],),
        in_specs=[pl.BlockSpec((tm, d_exp), x_map),
                  pl.BlockSpec((None, None, d_exp, tn), w_map)],
        out_specs=pl.BlockSpec((tm, tn), o_map))
    return pl.pallas_call(
        _expert_down_body, grid_spec=grid_spec,
        out_shape=jax.ShapeDtypeStruct((n_rows, d), F32),
        compiler_params=_params(("arbitrary",)), name=name,
    )(*steps, h, w_down)


def _combine_body(p1_ref, p2_ref, y_hbm, x_ref, g1_ref, g2_ref, lg_ref, lb_ref, o_ref, ob_ref,
                  buf1, buf2, sem, *, tm, alpha):
    step = pl.program_id(0)
    n_steps = pl.num_programs(0)

    def issue(tile, slot):
        base = tile * tm

        def start(i, c):
            _row_copy(y_hbm, p1_ref[base + i], buf1.at[slot], i, sem.at[slot]).start()
            _row_copy(y_hbm, p2_ref[base + i], buf2.at[slot], i, sem.at[slot]).start()
            return c

        lax.fori_loop(0, tm, start, 0, unroll=GATHER_UNROLL)

    @pl.when(step == 0)
    def _():
        issue(0, 0)

    @pl.when(step + 1 < n_steps)
    def _():
        issue(step + 1, (step + 1) % 2)

    slot = step % 2

    def wait(i, c):
        _row_copy(y_hbm, 0, buf1.at[slot], i, sem.at[slot]).wait()
        _row_copy(y_hbm, 0, buf2.at[slot], i, sem.at[slot]).wait()
        return c

    lax.fori_loop(0, tm, wait, 0, unroll=GATHER_UNROLL)
    moe = g1_ref[...] * buf1[slot] + g2_ref[...] * buf2[slot]
    r = _ln(alpha * x_ref[...] + moe, lg_ref[...], lb_ref[...])
    o_ref[...] = r
    ob_ref[...] = r.astype(BF16)


def _combine(pos1, pos2, y, x, g1, g2, ln_g, ln_b, layer, alpha, *, tm, name):
    m, d = x.shape
    row = lambda i, p1, p2: (i, 0)
    grid_spec = pltpu.PrefetchScalarGridSpec(
        num_scalar_prefetch=2, grid=(m // tm,),
        in_specs=[pl.BlockSpec(memory_space=pl.ANY),
                  pl.BlockSpec((tm, d), row),
                  pl.BlockSpec((tm, 1), row),
                  pl.BlockSpec((tm, 1), row),
                  pl.BlockSpec((None, 1, d), lambda i, p1, p2: (layer, 0, 0)),
                  pl.BlockSpec((None, 1, d), lambda i, p1, p2: (layer, 0, 0))],
        out_specs=[pl.BlockSpec((tm, d), row), pl.BlockSpec((tm, d), row)],
        scratch_shapes=[pltpu.VMEM((2, tm, d), y.dtype), pltpu.VMEM((2, tm, d), y.dtype),
                        pltpu.SemaphoreType.DMA((2,))])
    return pl.pallas_call(
        functools.partial(_combine_body, tm=tm, alpha=alpha), grid_spec=grid_spec,
        out_shape=[jax.ShapeDtypeStruct((m, d), F32), jax.ShapeDtypeStruct((m, d), BF16)],
        compiler_params=_params(("arbitrary",)), name=name,
    )(pos1, pos2, y, x, g1, g2, ln_g, ln_b)


def _dispatch_plan(e1, e2, n_experts, tm, n_chunks_up, n_chunks_down):
    t = e1.shape[0]
    e_flat = jnp.concatenate([e1.reshape(t), e2.reshape(t)])
    onehot = (e_flat[:, None] == jnp.arange(n_experts, dtype=jnp.int32)[None, :]).astype(jnp.int32)
    csum = jnp.cumsum(onehot, axis=0)
    counts = csum[-1]
    rank = jnp.take_along_axis(csum, e_flat[:, None], axis=1)[:, 0] - 1
    tiles_e = (counts + tm - 1) // tm
    tile_end = jnp.cumsum(tiles_e)
    tile_start = tile_end - tiles_e
    n_tiles = tile_end[-1]
    n_tiles_max = (2 * t + tm - 1) // tm + n_experts
    n_rows_max = n_tiles_max * tm
    dest = tile_start[e_flat] * tm + rank
    token = jnp.concatenate([jnp.arange(t, dtype=jnp.int32)] * 2)
    row_token = jnp.zeros((n_rows_max,), jnp.int32).at[dest].set(token)

    def steps(n_chunks):
        n_steps = n_tiles_max * n_chunks
        n_valid = n_tiles * n_chunks
        s = jnp.arange(n_steps, dtype=jnp.int32)
        valid = s < n_valid
        sc = jnp.minimum(s, n_valid - 1)
        exp = jnp.sum(sc[:, None] >= (tile_end * n_chunks)[None, :], axis=1).astype(jnp.int32)
        exp = jnp.minimum(exp, n_experts - 1)
        local = sc - tile_start[exp] * n_chunks
        nt = jnp.maximum(tiles_e[exp], 1)
        wchunk = local // nt
        tile = tile_start[exp] + local % nt
        pad = s - n_valid
        tile = jnp.where(valid, tile, n_tiles + pad // n_chunks)
        ochunk = jnp.where(valid, wchunk, pad % n_chunks)
        i32 = lambda a: a.astype(jnp.int32)
        return (i32(tile), i32(exp), i32(wchunk), i32(ochunk), i32(valid))

    return row_token, dest[:t], dest[t:], steps(n_chunks_up), steps(n_chunks_down)


def kernel(x_prompt, x_sample, cache_ckv, cache_krope, page_table, w_in, q_norm_g, kv_norm_g, w_qb, w_kvb, w_oa, sgu_ln_g, sgu_ln_b, w_s, b_s, w_ob, w_out, ln1_g, ln1_b, w_router, b_router, w_exp_gate, w_exp_up, w_exp_down, ln2_g, ln2_b):
    batch, seq, d_model = x_prompt.shape
    n_seq, dec_seq, _ = x_sample.shape
    assert dec_seq == 1
    depth = w_in.shape[0]
    q_rank = q_norm_g.shape[1]
    kv_rank = kv_norm_g.shape[1]
    rope_dim = cache_krope.shape[-1]
    n_heads = w_kvb.shape[2]
    nope_dim = w_qb.shape[2] // n_heads - rope_dim
    v_dim = w_kvb.shape[3] - nope_dim
    d_sgu = sgu_ln_g.shape[1]
    n_groups, chunk = w_s.shape[1], w_s.shape[2]
    n_experts = w_router.shape[1]
    d_exp = w_exp_gate.shape[3]
    n_pages = page_table.shape[1]
    past_len = n_pages * cache_ckv.shape[2]
    assert nope_dim == LANES and v_dim == LANES and 2 * rope_dim == LANES
    assert d_sgu == n_groups * LANES and chunk == LANES and seq % chunk == 0
    half = rope_dim // 2
    alpha = float((2 * depth) ** 0.25)
    scale = float((nope_dim + rope_dim) ** -0.5)

    n_prompt = batch * seq
    m = n_prompt + n_seq
    tm = _row_tile(m, ROW_TILE)

    c0 = q_rank + kv_rank
    c1 = c0 + rope_dim
    c2 = c1 + d_sgu
    c3 = c2 + d_sgu
    w_in_t = jnp.swapaxes(w_in, 1, 2)
    w_small = jnp.concatenate(
        [w_in_t[:, :c1], -w_in_t[:, c0 + half:c1], w_in_t[:, c0:c0 + half]], axis=1).astype(BF16)
    w_u = w_in_t[:, c1:c2].astype(BF16)
    w_v = w_in_t[:, c2:c3].astype(BF16)
    w_g = w_in_t[:, c3:].astype(BF16)
    cache_krope_t = jnp.swapaxes(cache_krope, 2, 3)
    wq = w_qb.reshape(depth, q_rank, n_heads, nope_dim + rope_dim)
    wq_r = wq[..., nope_dim:]
    wq_ext = jnp.concatenate(
        [wq, -wq_r[..., half:], wq_r[..., :half]], axis=-1
    ).reshape(depth, q_rank, n_heads * 2 * LANES).astype(BF16)
    wk_flat = w_kvb[..., :nope_dim].reshape(depth, kv_rank, n_heads * LANES).astype(BF16)
    wv_flat = w_kvb[..., nope_dim:].reshape(depth, kv_rank, n_heads * LANES).astype(BF16)
    wk_t = jnp.transpose(w_kvb[..., :nope_dim], (0, 2, 3, 1)).astype(BF16)
    wv_h = jnp.transpose(w_kvb[..., nope_dim:], (0, 2, 1, 3)).astype(BF16)
    w_oa_b = w_oa.astype(BF16)
    w_ob_b = w_ob.astype(BF16)
    w_out_b = w_out.astype(BF16)
    w_s_first = jnp.repeat(w_s[:, :, 0, 0], LANES, axis=1).reshape(depth, 1, d_sgu)
    b_s_first = jnp.repeat(b_s[:, :, 0], LANES, axis=1).reshape(depth, 1, d_sgu)
    b_s_t = jnp.transpose(b_s, (0, 2, 1))
    g3 = lambda a: a.reshape(depth, 1, a.shape[1])
    q_norm3, kv_norm3 = g3(q_norm_g), g3(kv_norm_g)
    sgu_g3, sgu_b3 = g3(sgu_ln_g), g3(sgu_ln_b)
    ln1_g3, ln1_b3, ln2_g3, ln2_b3 = g3(ln1_g), g3(ln1_b), g3(ln2_g), g3(ln2_b)

    pos = jnp.concatenate([jnp.tile(jnp.arange(seq, dtype=jnp.int32), batch),
                           jnp.full((n_seq,), past_len, jnp.int32)])
    inv = ROPE_THETA ** (-jnp.arange(half, dtype=F32) / half)
    ang = pos.astype(F32)[:, None] * inv[None, :]
    cs = jnp.concatenate([jnp.cos(ang), jnp.cos(ang), jnp.sin(ang), jnp.sin(ang)], axis=1)

    x = jnp.concatenate([x_prompt.reshape(n_prompt, d_model), x_sample.reshape(n_seq, d_model)])
    xb = x.astype(BF16)

    n_small = w_small.shape[1]
    tn_q = _col_tile(wq_ext.shape[-1], COL_TILE)
    heads_q = tn_q // (2 * LANES)
    tn_d = _col_tile(d_model, COL_TILE)
    up_chunks = d_exp // _col_tile(d_exp, EXPERT_COL_TILE)
    down_chunks = d_model // _col_tile(d_model, DOWN_COL_TILE)
    cs_spec = (cs, (tm, LANES), lambda i, j: (i, 0))

    ckv_p, kr_p, ckv_d, kr_d, v_d = [], [], [], [], []
    for l in range(depth):
        qn, ckv, kr, kr_pad = _matmul(
            xb, w_small, l, tm=tm, tn=n_small, w_transposed=True,
            epi=functools.partial(_epi_small, q_rank=q_rank, kv_rank=kv_rank),
            extras=[cs_spec,
                    (q_norm3, (None, 1, q_rank), lambda i, j: (l, 0, 0)),
                    (kv_norm3, (None, 1, kv_rank), lambda i, j: (l, 0, 0))],
            outs=[(q_rank, q_rank, BF16), (kv_rank, kv_rank, F32), (rope_dim, rope_dim, F32),
                  (LANES, LANES, BF16)],
            name=f"in_small_{l}")
        (u,) = _matmul(xb, w_u, l, tm=tm, tn=_col_tile(d_sgu, COL_TILE), epi=_epi_gelu,
                       outs=[(d_sgu, _col_tile(d_sgu, COL_TILE), BF16)], w_transposed=True,
                       name=f"in_u_{l}")
        (v_pre,) = _matmul(xb, w_v, l, tm=tm, tn=_col_tile(d_sgu, COL_TILE), epi=_epi_gelu,
                           outs=[(d_sgu, _col_tile(d_sgu, COL_TILE), F32)], w_transposed=True,
                           name=f"in_v_{l}")
        v = _ln_rows(v_pre, sgu_g3, sgu_b3, l, tm=_row_tile(m, 512), name=f"sgu_ln_{l}")
        (gates,) = _matmul(xb, w_g, l, tm=tm, tn=tn_d, epi=_epi_sigmoid,
                           outs=[(2 * d_model, tn_d, BF16)], w_transposed=True,
                           name=f"in_gates_{l}")

        (q_full,) = _matmul(qn, wq_ext, l, tm=tm, tn=tn_q,
                            epi=functools.partial(_epi_q, heads=heads_q, scale=scale),
                            extras=[cs_spec],
                            outs=[(n_heads * 2 * LANES, tn_q, BF16)], name=f"q_proj_{l}")
        k_full, v_full = _kv_expand(ckv, kr_pad, wk_flat, wv_flat, l, n_prompt,
                                    tm=_row_tile(n_prompt, 1024),
                                    heads_per_step=min(n_heads, 8), name=f"kv_expand_{l}")
        o_prompt = _prompt_attn(q_full, k_full, v_full, batch, seq,
                                tq=min(seq, ATTN_Q_TILE), name=f"prompt_attn_{l}")
        q_dec = q_full[n_prompt:]
        q_lat = _headwise(q_dec, wk_t, l, in_stride=2 * LANES,
                          heads_per_step=min(n_heads, 8), name=f"q_absorb_{l}")
        q_rope_dec = q_dec.reshape(n_seq, n_heads, 2 * LANES)[:, :, LANES:]
        o_lat = _paged_attn(page_table, q_lat.reshape(n_seq, n_heads, kv_rank), q_rope_dec,
                            ckv[n_prompt:].reshape(n_seq, 1, kv_rank),
                            kr[n_prompt:].reshape(n_seq, 1, rope_dim),
                            cache_ckv, cache_krope_t, l, name=f"paged_attn_{l}")
        o_dec = _headwise(o_lat.reshape(n_seq, n_heads * kv_rank), wv_h, l, in_stride=kv_rank,
                          heads_per_step=min(n_heads, 8), name=f"v_up_{l}")
        o = jnp.concatenate([o_prompt, o_dec])

        ub_prompt = _sgu_prompt(u, v, w_s, b_s_t, l, n_prompt, name=f"sgu_prompt_{l}")
        ub_dec = _sgu_sample(u[n_prompt:], v[n_prompt:], w_s_first, b_s_first, l,
                             name=f"sgu_sample_{l}")
        ub = jnp.concatenate([ub_prompt, ub_dec])

        h = _gate_merge(o, ub, w_oa_b, w_ob_b, gates, l, tm=tm, tn=_col_tile(d_model, 512),
                        name=f"gate_merge_{l}")
        (mix,) = _matmul(h, w_out_b, l, tm=tm, tn=tn_d, epi=_epi_id,
                         outs=[(d_model, tn_d, F32)], name=f"out_proj_{l}")
        x1, x1b = _ln_residual(x, mix, ln1_g3, ln1_b3, l, alpha, tm=_row_tile(m, 128),
                               name=f"ln1_{l}")

        e1, e2, g1, g2 = _router(x1b, w_router, b_router, tm=tm, name=f"router_{l}")
        row_token, pos1, pos2, steps_up, steps_down = _dispatch_plan(
            e1, e2, n_experts, EXPERT_ROW_TILE, up_chunks, down_chunks)
        xs = _gather_rows(x1, row_token, tm=EXPERT_ROW_TILE, name=f"moe_gather_{l}")
        hid = _expert_up(steps_up, xs, w_exp_gate, w_exp_up, l, tm=EXPERT_ROW_TILE,
                         tn=d_exp // up_chunks, name=f"expert_up_{l}")
        y = _expert_down(steps_down, hid, w_exp_down, l, tm=EXPERT_ROW_TILE,
                         tn=d_model // down_chunks, name=f"expert_down_{l}")
        x, xb = _combine(pos1, pos2, y, x1, g1, g2, ln2_g3, ln2_b3, l, alpha,
                         tm=_row_tile(m, 128), name=f"moe_combine_{l}")

        ckv_p.append(ckv[:n_prompt].reshape(batch, seq, kv_rank))
        kr_p.append(kr[:n_prompt].reshape(batch, seq, rope_dim))
        ckv_d.append(ckv[n_prompt:].reshape(n_seq, 1, kv_rank))
        kr_d.append(kr[n_prompt:].reshape(n_seq, 1, rope_dim))
        v_d.append(v[n_prompt:].reshape(n_seq, 1, d_sgu))

    return (x[:n_prompt].reshape(batch, seq, d_model), x[n_prompt:].reshape(n_seq, 1, d_model),
            jnp.stack(ckv_p), jnp.stack(kr_p), jnp.stack(ckv_d), jnp.stack(kr_d), jnp.stack(v_d))
```

```python
import functools

import jax
import jax.numpy as jnp
from jax import lax
from jax.experimental import pallas as pl
from jax.experimental.pallas import tpu as pltpu

F32 = jnp.float32
BF16 = jnp.bfloat16

ROPE_THETA = 10000.0
LN_EPS = 1e-5
RMS_EPS = 1e-6
N_EXPERT_GROUPS = 4

LANES = 128
SUBLANES = 8
VMEM_LIMIT_BYTES = 52 * 1024 * 1024

ROW_TILE = 640
COL_TILE = 1024
EXPERT_ROW_TILE = 128
EXPERT_COL_TILE = 512
DOWN_COL_TILE = 2048
ATTN_Q_TILE = 512
GATHER_UNROLL = 8


def _params(semantics):
    return pltpu.CompilerParams(dimension_semantics=semantics, vmem_limit_bytes=VMEM_LIMIT_BYTES)


def _row_tile(m, target):
    best = None
    for d in range(SUBLANES, min(m, target) + 1, SUBLANES):
        if m % d == 0:
            best = d
    return best if best is not None else m


def _col_tile(n, target):
    best = None
    for d in range(LANES, min(n, target) + 1, LANES):
        if n % d == 0:
            best = d
    return best if best is not None else n


def _mm_body(*refs, n_extra, n_out, epi, w_transposed):
    x_ref, w_ref = refs[0], refs[1]
    extra = refs[2:2 + n_extra]
    outs = refs[2 + n_extra:2 + n_extra + n_out]
    if w_transposed:
        acc = lax.dot_general(x_ref[...], w_ref[...], (((1,), (1,)), ((), ())),
                              preferred_element_type=F32)
    else:
        acc = jnp.dot(x_ref[...], w_ref[...], preferred_element_type=F32)
    res = epi(acc, *[e[...] for e in extra])
    if not isinstance(res, tuple):
        res = (res,)
    for o_ref, r in zip(outs, res):
        o_ref[...] = r.astype(o_ref.dtype)


def _matmul(x, w, layer, *, tm, tn, epi, extras=(), outs, name, w_transposed=False):
    m, k = x.shape
    n = w.shape[1] if w_transposed else w.shape[2]
    grid = (m // tm, n // tn)
    if w_transposed:
        w_spec = pl.BlockSpec((None, tn, k), lambda i, j: (layer, j, 0))
    else:
        w_spec = pl.BlockSpec((None, k, tn), lambda i, j: (layer, 0, j))
    in_specs = [pl.BlockSpec((tm, k), lambda i, j: (i, 0)), w_spec]
    in_specs += [pl.BlockSpec(bs, im) for (_, bs, im) in extras]
    out_specs = [pl.BlockSpec((tm, ct), lambda i, j: (i, j)) for (_, ct, _) in outs]
    out_shape = [jax.ShapeDtypeStruct((m, c), dt) for (c, _, dt) in outs]
    body = functools.partial(_mm_body, n_extra=len(extras), n_out=len(outs), epi=epi,
                             w_transposed=w_transposed)
    res = pl.pallas_call(
        body, grid=grid, in_specs=in_specs, out_specs=out_specs, out_shape=out_shape,
        compiler_params=_params(("parallel", "arbitrary")), name=name,
    )(x, w, *[a for (a, _, _) in extras])
    return res


def _rope_pair(t, cs):
    tt = t * cs
    return tt + pltpu.roll(tt, LANES // 2, 1)


def _epi_small(acc, cs, gq, gkv, *, q_rank, kv_rank):
    qa = acc[:, :q_rank]
    ca = acc[:, q_rank:q_rank + kv_rank]
    t = acc[:, q_rank + kv_rank:]
    qn = qa * lax.rsqrt(jnp.mean(qa * qa, axis=-1, keepdims=True) + RMS_EPS) * gq
    ckv = ca * lax.rsqrt(jnp.mean(ca * ca, axis=-1, keepdims=True) + RMS_EPS) * gkv
    r = _rope_pair(t, cs)
    lane = lax.broadcasted_iota(jnp.int32, r.shape, 1)
    kr_pad = jnp.where(lane < LANES // 2, r, 0.0)
    return qn, ckv, r[:, :LANES // 2], kr_pad


def _epi_q(acc, cs, *, heads, scale):
    parts = []
    for h in range(heads):
        blk = acc[:, h * 2 * LANES:(h + 1) * 2 * LANES]
        parts.append(blk[:, :LANES] * scale)
        parts.append(_rope_pair(blk[:, LANES:], cs) * scale)
    return jnp.concatenate(parts, axis=1)


def _epi_gelu(acc):
    return jax.nn.gelu(acc)


def _epi_sigmoid(acc):
    return jax.nn.sigmoid(acc)


def _epi_id(acc):
    return acc


def _ln(x, g, b):
    mu = jnp.mean(x, axis=-1, keepdims=True)
    xc = x - mu
    var = jnp.mean(xc * xc, axis=-1, keepdims=True)
    return xc * lax.rsqrt(var + LN_EPS) * g + b


def _ln_rows_body(x_ref, g_ref, b_ref, o_ref):
    o_ref[...] = _ln(x_ref[...], g_ref[...], b_ref[...])


def _ln_rows(x, g, b, layer, *, tm, name):
    m, n = x.shape
    return pl.pallas_call(
        _ln_rows_body, grid=(m // tm,),
        in_specs=[pl.BlockSpec((tm, n), lambda i: (i, 0)),
                  pl.BlockSpec((None, 1, n), lambda i: (layer, 0, 0)),
                  pl.BlockSpec((None, 1, n), lambda i: (layer, 0, 0))],
        out_specs=pl.BlockSpec((tm, n), lambda i: (i, 0)),
        out_shape=jax.ShapeDtypeStruct((m, n), F32),
        compiler_params=_params(("parallel",)), name=name,
    )(x, g, b)


def _ln_residual_body(x_ref, y_ref, g_ref, b_ref, o_ref, ob_ref, *, alpha):
    r = _ln(alpha * x_ref[...] + y_ref[...], g_ref[...], b_ref[...])
    o_ref[...] = r
    ob_ref[...] = r.astype(BF16)


def _ln_residual(x, y, g, b, layer, alpha, *, tm, name):
    m, n = x.shape
    return pl.pallas_call(
        functools.partial(_ln_residual_body, alpha=alpha), grid=(m // tm,),
        in_specs=[pl.BlockSpec((tm, n), lambda i: (i, 0)),
                  pl.BlockSpec((tm, n), lambda i: (i, 0)),
                  pl.BlockSpec((None, 1, n), lambda i: (layer, 0, 0)),
                  pl.BlockSpec((None, 1, n), lambda i: (layer, 0, 0))],
        out_specs=[pl.BlockSpec((tm, n), lambda i: (i, 0)),
                   pl.BlockSpec((tm, n), lambda i: (i, 0))],
        out_shape=[jax.ShapeDtypeStruct((m, n), F32), jax.ShapeDtypeStruct((m, n), BF16)],
        compiler_params=_params(("parallel",)), name=name,
    )(x, y, g, b)


def _kv_expand_body(c_ref, kr_ref, wk_ref, wv_ref, k_ref, v_ref, *, heads):
    c = c_ref[...].astype(BF16)
    kn = jnp.dot(c, wk_ref[...], preferred_element_type=F32).astype(BF16)
    kr = kr_ref[...]
    for h in range(heads):
        k_ref[:, h * 2 * LANES:h * 2 * LANES + LANES] = kn[:, h * LANES:(h + 1) * LANES]
        k_ref[:, h * 2 * LANES + LANES:(h + 1) * 2 * LANES] = kr
    v_ref[...] = jnp.dot(c, wv_ref[...], preferred_element_type=F32).astype(BF16)


def _kv_expand(ckv, kr_pad, wk, wv, layer, n_rows, *, tm, heads_per_step, name):
    kv_rank = ckv.shape[1]
    n_heads = wk.shape[-1] // LANES
    hb = heads_per_step
    return pl.pallas_call(
        functools.partial(_kv_expand_body, heads=hb), grid=(n_rows // tm, n_heads // hb),
        in_specs=[pl.BlockSpec((tm, kv_rank), lambda i, j: (i, 0)),
                  pl.BlockSpec((tm, LANES), lambda i, j: (i, 0)),
                  pl.BlockSpec((None, kv_rank, hb * LANES), lambda i, j: (layer, 0, j)),
                  pl.BlockSpec((None, kv_rank, hb * LANES), lambda i, j: (layer, 0, j))],
        out_specs=[pl.BlockSpec((tm, hb * 2 * LANES), lambda i, j: (i, j)),
                   pl.BlockSpec((tm, hb * LANES), lambda i, j: (i, j))],
        out_shape=[jax.ShapeDtypeStruct((n_rows, n_heads * 2 * LANES), BF16),
                   jax.ShapeDtypeStruct((n_rows, n_heads * LANES), BF16)],
        compiler_params=_params(("parallel", "arbitrary")), name=name,
    )(ckv, kr_pad, wk, wv)


def _prompt_attn_body(q_ref, k_ref, v_ref, o_ref, *, tq, seq):
    nt = (((1,), (1,)), ((), ()))
    row = lax.broadcasted_iota(jnp.int32, (tq, tq), 0)
    col = lax.broadcasted_iota(jnp.int32, (tq, tq), 1)
    causal = col <= row
    for qi in range(seq // tq):
        off, kend = qi * tq, (qi + 1) * tq
        q = q_ref[off:kend, :]
        sd = lax.dot_general(q, k_ref[off:kend, :], nt, preferred_element_type=F32)
        sd = jnp.where(causal, sd, -jnp.inf)
        m = jnp.max(sd, axis=-1, keepdims=True)
        if qi > 0:
            so = lax.dot_general(q, k_ref[0:off, :], nt, preferred_element_type=F32)
            m = jnp.maximum(m, jnp.max(so, axis=-1, keepdims=True))
            po = jnp.exp(so - m)
        pd = jnp.exp(sd - m)
        l = jnp.sum(pd, axis=-1, keepdims=True)
        o = jnp.dot(pd.astype(BF16), v_ref[off:kend, :], preferred_element_type=F32)
        if qi > 0:
            l = l + jnp.sum(po, axis=-1, keepdims=True)
            o = o + jnp.dot(po.astype(BF16), v_ref[0:off, :], preferred_element_type=F32)
        o_ref[off:kend, :] = (o / l).astype(o_ref.dtype)


def _prompt_attn(q, k, v, batch, seq, *, tq, name):
    n_heads = v.shape[1] // LANES
    return pl.pallas_call(
        functools.partial(_prompt_attn_body, tq=tq, seq=seq), grid=(batch, n_heads),
        in_specs=[pl.BlockSpec((seq, 2 * LANES), lambda b, h: (b, h)),
                  pl.BlockSpec((seq, 2 * LANES), lambda b, h: (b, h)),
                  pl.BlockSpec((seq, LANES), lambda b, h: (b, h))],
        out_specs=pl.BlockSpec((seq, LANES), lambda b, h: (b, h)),
        out_shape=jax.ShapeDtypeStruct((batch * seq, n_heads * LANES), BF16),
        compiler_params=_params(("parallel", "parallel")), name=name,
    )(q, k, v)


def _headwise_body(x_ref, w_ref, o_ref, *, heads, in_stride, d_in, d_out):
    for h in range(heads):
        xh = x_ref[:, h * in_stride:h * in_stride + d_in]
        o_ref[:, h * d_out:(h + 1) * d_out] = jnp.dot(
            xh, w_ref[h], preferred_element_type=F32).astype(o_ref.dtype)


def _headwise(x, w, layer, *, in_stride, heads_per_step, name):
    m = x.shape[0]
    n_heads, d_in, d_out = w.shape[1:]
    hb = heads_per_step
    return pl.pallas_call(
        functools.partial(_headwise_body, heads=hb, in_stride=in_stride, d_in=d_in, d_out=d_out),
        grid=(n_heads // hb,),
        in_specs=[pl.BlockSpec((m, hb * in_stride), lambda j: (0, j)),
                  pl.BlockSpec((None, hb, d_in, d_out), lambda j: (layer, j, 0, 0))],
        out_specs=pl.BlockSpec((m, hb * d_out), lambda j: (0, j)),
        out_shape=jax.ShapeDtypeStruct((m, n_heads * d_out), BF16),
        compiler_params=_params(("parallel",)), name=name,
    )(x, w)


def _paged_attn_body(pt_ref, ql_ref, qr_ref, cn_ref, rn_ref, c_hbm, r_hbm, o_ref,
                     cbuf, rbuf, sem, *, n_pages, rope_dim, layer):
    b = pl.program_id(0)
    slot = b % 2

    def page_copies(seq, i, s):
        pg = pt_ref[seq, i]
        return (pltpu.make_async_copy(c_hbm.at[layer, pg], cbuf.at[s, i], sem.at[s]),
                pltpu.make_async_copy(r_hbm.at[layer, pg], rbuf.at[s, i], sem.at[s]))

    def issue(seq, s):
        def body(i, c):
            for cp in page_copies(seq, i, s):
                cp.start()
            return c

        lax.fori_loop(0, n_pages, body, 0, unroll=GATHER_UNROLL)

    @pl.when(b == 0)
    def _():
        issue(0, 0)

    def wait(i, c):
        for cp in page_copies(b, i, slot):
            cp.wait()
        return c

    lax.fori_loop(0, n_pages, wait, 0, unroll=GATHER_UNROLL)

    @pl.when(b + 1 < pl.num_programs(0))
    def _():
        issue(b + 1, 1 - slot)

    ql = ql_ref[...]
    qr = qr_ref[...][:, :rope_dim]
    nt = (((1,), (1,)), ((), ()))
    ss = []
    for i in range(n_pages):
        c = cbuf[slot, i].astype(BF16)
        r_t = rbuf[slot, i].astype(BF16)
        ss.append(lax.dot_general(ql, c, nt, preferred_element_type=F32)
                  + jnp.dot(qr, r_t, preferred_element_type=F32))
    s = jnp.concatenate(ss, axis=1)
    cn = cn_ref[...]
    rn = rn_ref[...]
    s_self = (jnp.sum(ql.astype(F32) * cn, axis=-1, keepdims=True)
              + jnp.sum(qr.astype(F32) * rn, axis=-1, keepdims=True))
    m = jnp.maximum(jnp.max(s, axis=-1, keepdims=True), s_self)
    p = jnp.exp(s - m)
    p_self = jnp.exp(s_self - m)
    l = jnp.sum(p, axis=-1, keepdims=True) + p_self
    pv = p_self * cn
    for i in range(n_pages):
        pv = pv + jnp.dot(p[:, i * LANES:(i + 1) * LANES].astype(BF16),
                          cbuf[slot, i].astype(BF16), preferred_element_type=F32)
    o_ref[...] = (pv / l).astype(o_ref.dtype)


def _paged_attn(page_table, q_lat, q_rope, c_new, r_new, cache_c, cache_r_t, layer, *, name):
    n_seq, n_heads, kv_rank = q_lat.shape
    n_pages = page_table.shape[1]
    page = cache_c.shape[2]
    rope_dim = cache_r_t.shape[2]
    assert page == LANES
    seq_map = lambda b, pt: (b, 0, 0)
    grid_spec = pltpu.PrefetchScalarGridSpec(
        num_scalar_prefetch=1, grid=(n_seq,),
        in_specs=[pl.BlockSpec((None, n_heads, kv_rank), seq_map),
                  pl.BlockSpec((None, n_heads, LANES), seq_map),
                  pl.BlockSpec((None, 1, kv_rank), seq_map),
                  pl.BlockSpec((None, 1, rope_dim), seq_map),
                  pl.BlockSpec(memory_space=pl.ANY),
                  pl.BlockSpec(memory_space=pl.ANY)],
        out_specs=pl.BlockSpec((None, n_heads, kv_rank), seq_map),
        scratch_shapes=[pltpu.VMEM((2, n_pages, page, kv_rank), cache_c.dtype),
                        pltpu.VMEM((2, n_pages, rope_dim, page), cache_r_t.dtype),
                        pltpu.SemaphoreType.DMA((2,))])
    return pl.pallas_call(
        functools.partial(_paged_attn_body, n_pages=n_pages, rope_dim=rope_dim, layer=layer),
        grid_spec=grid_spec,
        out_shape=jax.ShapeDtypeStruct((n_seq, n_heads, kv_rank), BF16),
        compiler_params=_params(("arbitrary",)), name=name,
    )(page_table, q_lat, q_rope, c_new, r_new, cache_c, cache_r_t)


def _sgu_prompt_body(u_ref, v_ref, w_ref, bt_ref, o_ref, *, groups):
    chunk = w_ref.shape[-1]
    row = lax.broadcasted_iota(jnp.int32, (chunk, chunk), 0)
    col = lax.broadcasted_iota(jnp.int32, (chunk, chunk), 1)
    tril = col <= row
    for g in range(groups):
        w = jnp.where(tril, w_ref[g], 0.0).astype(BF16)
        vg = v_ref[:, g * LANES:(g + 1) * LANES].astype(BF16)
        mix = jnp.dot(w, vg, preferred_element_type=F32) + bt_ref[:, g:g + 1]
        ug = u_ref[:, g * LANES:(g + 1) * LANES].astype(F32)
        o_ref[:, g * LANES:(g + 1) * LANES] = (ug * mix).astype(o_ref.dtype)


def _sgu_prompt(u, v, w_s, b_s_t, layer, n_rows, *, name):
    groups, chunk = w_s.shape[1], w_s.shape[2]
    d_sgu = u.shape[1]
    return pl.pallas_call(
        functools.partial(_sgu_prompt_body, groups=groups), grid=(n_rows // chunk,),
        in_specs=[pl.BlockSpec((chunk, d_sgu), lambda i: (i, 0)),
                  pl.BlockSpec((chunk, d_sgu), lambda i: (i, 0)),
                  pl.BlockSpec((None, groups, chunk, chunk), lambda i: (layer, 0, 0, 0)),
                  pl.BlockSpec((None, chunk, groups), lambda i: (layer, 0, 0))],
        out_specs=pl.BlockSpec((chunk, d_sgu), lambda i: (i, 0)),
        out_shape=jax.ShapeDtypeStruct((n_rows, d_sgu), BF16),
        compiler_params=_params(("parallel",)), name=name,
    )(u, v, w_s, b_s_t)


def _sgu_sample_body(u_ref, v_ref, w_ref, b_ref, o_ref):
    mix = v_ref[...] * w_ref[...] + b_ref[...]
    o_ref[...] = (u_ref[...].astype(F32) * mix).astype(o_ref.dtype)


def _sgu_sample(u, v, w_row, b_row, layer, *, name):
    m, d_sgu = u.shape
    return pl.pallas_call(
        _sgu_sample_body, grid=(1,),
        in_specs=[pl.BlockSpec((m, d_sgu), lambda i: (0, 0)),
                  pl.BlockSpec((m, d_sgu), lambda i: (0, 0)),
                  pl.BlockSpec((None, 1, d_sgu), lambda i: (layer, 0, 0)),
                  pl.BlockSpec((None, 1, d_sgu), lambda i: (layer, 0, 0))],
        out_specs=pl.BlockSpec((m, d_sgu), lambda i: (0, 0)),
        out_shape=jax.ShapeDtypeStruct((m, d_sgu), BF16),
        compiler_params=_params(("arbitrary",)), name=name,
    )(u, v, w_row, b_row)


def _gate_merge_body(o_ref, ub_ref, wa_ref, wb_ref, ga_ref, gb_ref, h_ref):
    ya = jnp.dot(o_ref[...], wa_ref[...], preferred_element_type=F32)
    yb = jnp.dot(ub_ref[...], wb_ref[...], preferred_element_type=F32)
    h_ref[...] = (ga_ref[...].astype(F32) * ya + gb_ref[...].astype(F32) * yb).astype(h_ref.dtype)


def _gate_merge(o, ub, w_oa, w_ob, gates, layer, *, tm, tn, name):
    m, ka = o.shape
    kb = ub.shape[1]
    n = w_oa.shape[-1]
    nj = n // tn
    return pl.pallas_call(
        _gate_merge_body, grid=(m // tm, nj),
        in_specs=[pl.BlockSpec((tm, ka), lambda i, j: (i, 0)),
                  pl.BlockSpec((tm, kb), lambda i, j: (i, 0)),
                  pl.BlockSpec((None, ka, tn), lambda i, j: (layer, 0, j)),
                  pl.BlockSpec((None, kb, tn), lambda i, j: (layer, 0, j)),
                  pl.BlockSpec((tm, tn), lambda i, j: (i, j)),
                  pl.BlockSpec((tm, tn), lambda i, j: (i, j + nj))],
        out_specs=pl.BlockSpec((tm, tn), lambda i, j: (i, j)),
        out_shape=jax.ShapeDtypeStruct((m, n), BF16),
        compiler_params=_params(("parallel", "arbitrary")), name=name,
    )(o, ub, w_oa, w_ob, gates, gates)


def _router_body(x_ref, w_ref, b_ref, e1_ref, e2_ref, g1_ref, g2_ref, *, n_experts):
    logits = jnp.dot(x_ref[...], w_ref[...].astype(BF16), preferred_element_type=F32)
    scores = jax.nn.sigmoid(logits)
    sel = scores + b_ref[...]
    per_group = n_experts // N_EXPERT_GROUPS
    lane = lax.broadcasted_iota(jnp.int32, sel.shape, 1).astype(F32)
    neg = -jnp.inf

    def top2(vals):
        m1 = jnp.max(vals, axis=-1, keepdims=True)
        i1 = jnp.min(jnp.where(vals == m1, lane, float(n_experts)), axis=-1, keepdims=True)
        rest = jnp.where(lane == i1, neg, vals)
        m2 = jnp.max(rest, axis=-1, keepdims=True)
        i2 = jnp.min(jnp.where(rest == m2, lane, float(n_experts)), axis=-1, keepdims=True)
        return m1, i1, m2, i2

    best = None
    best_g = None
    for g in range(N_EXPERT_GROUPS):
        in_g = (lane >= g * per_group) & (lane < (g + 1) * per_group)
        m1, _, m2, _ = top2(jnp.where(in_g, sel, neg))
        val = m1 + m2
        if g == 0:
            best, best_g = val, jnp.zeros(val.shape, F32)
        else:
            upd = val > best
            best_g = jnp.where(upd, float(g), best_g)
            best = jnp.where(upd, val, best)
    lo = best_g * per_group
    in_best = (lane >= lo) & (lane < lo + per_group)
    _, e1, _, e2 = top2(jnp.where(in_best, sel, neg))
    w1 = jnp.sum(jnp.where(lane == e1, scores, 0.0), axis=-1, keepdims=True)
    w2 = jnp.sum(jnp.where(lane == e2, scores, 0.0), axis=-1, keepdims=True)
    tot = w1 + w2
    e1_ref[...] = e1.astype(jnp.int32)
    e2_ref[...] = e2.astype(jnp.int32)
    g1_ref[...] = w1 / tot
    g2_ref[...] = w2 / tot


def _router(xb, w_router, b_router, *, tm, name):
    m, d = xb.shape
    n_experts = w_router.shape[1]
    col = pl.BlockSpec((tm, 1), lambda i: (i, 0))
    return pl.pallas_call(
        functools.partial(_router_body, n_experts=n_experts), grid=(m // tm,),
        in_specs=[pl.BlockSpec((tm, d), lambda i: (i, 0)),
                  pl.BlockSpec((d, n_experts), lambda i: (0, 0)),
                  pl.BlockSpec((1, n_experts), lambda i: (0, 0))],
        out_specs=[col, col, col, col],
        out_shape=[jax.ShapeDtypeStruct((m, 1), jnp.int32), jax.ShapeDtypeStruct((m, 1), jnp.int32),
                   jax.ShapeDtypeStruct((m, 1), F32), jax.ShapeDtypeStruct((m, 1), F32)],
        compiler_params=_params(("parallel",)), name=name,
    )(xb, w_router, b_router.reshape(1, n_experts))


def _row_copy(src_hbm, row, dst, i, sem):
    return pltpu.make_async_copy(src_hbm.at[pl.ds(row, 1)], dst.at[pl.ds(i, 1)], sem)


def _gather_rows_body(tok_ref, x_hbm, o_ref, buf, sem, *, tm):
    step = pl.program_id(0)

    def issue(tile, slot):
        base = tile * tm

        def start(i, c):
            _row_copy(x_hbm, tok_ref[base + i], buf.at[slot], i, sem.at[slot]).start()
            return c

        lax.fori_loop(0, tm, start, 0, unroll=GATHER_UNROLL)

    @pl.when(step == 0)
    def _():
        issue(0, 0)

    @pl.when(step + 1 < pl.num_programs(0))
    def _():
        issue(step + 1, (step + 1) % 2)

    slot = step % 2

    def wait(i, c):
        _row_copy(x_hbm, 0, buf.at[slot], i, sem.at[slot]).wait()
        return c

    lax.fori_loop(0, tm, wait, 0, unroll=GATHER_UNROLL)
    o_ref[...] = buf[slot].astype(o_ref.dtype)


def _gather_rows(x, row_token, *, tm, name):
    d = x.shape[1]
    n_rows = row_token.shape[0]
    grid_spec = pltpu.PrefetchScalarGridSpec(
        num_scalar_prefetch=1, grid=(n_rows // tm,),
        in_specs=[pl.BlockSpec(memory_space=pl.ANY)],
        out_specs=pl.BlockSpec((tm, d), lambda i, tok: (i, 0)),
        scratch_shapes=[pltpu.VMEM((2, tm, d), x.dtype), pltpu.SemaphoreType.DMA((2,))])
    return pl.pallas_call(
        functools.partial(_gather_rows_body, tm=tm), grid_spec=grid_spec,
        out_shape=jax.ShapeDtypeStruct((n_rows, d), BF16),
        compiler_params=_params(("arbitrary",)), name=name,
    )(row_token, x)


def _weight_ring(s, exp_ref, wchunk_ref, grp_ref, first_ref, more_ref, nexp_ref, nchunk_ref,
                 copies):
    slot = grp_ref[s] % 2

    @pl.when(s == 0)
    def _():
        for cp in copies(exp_ref[0], wchunk_ref[0], 0):
            cp.start()

    @pl.when(first_ref[s] == 1)
    def _():
        for cp in copies(exp_ref[s], wchunk_ref[s], slot):
            cp.wait()

        @pl.when(more_ref[s] == 1)
        def _():
            for cp in copies(nexp_ref[s], nchunk_ref[s], 1 - slot):
                cp.start()

    return slot


def _expert_up_body(tile_ref, exp_ref, wchunk_ref, ochunk_ref, valid_ref, grp_ref, first_ref,
                    more_ref, nexp_ref, nchunk_ref, x_ref, wg_hbm, wu_hbm, h_ref,
                    wg_buf, wu_buf, sem, *, layer, tn):
    s = pl.program_id(0)

    def copies(e, c, slot):
        cols = pl.ds(pl.multiple_of(c * tn, tn), tn)
        return (pltpu.make_async_copy(wg_hbm.at[layer, e, :, cols], wg_buf.at[slot], sem.at[slot]),
                pltpu.make_async_copy(wu_hbm.at[layer, e, :, cols], wu_buf.at[slot], sem.at[slot]))

    slot = _weight_ring(s, exp_ref, wchunk_ref, grp_ref, first_ref, more_ref, nexp_ref,
                        nchunk_ref, copies)

    @pl.when(valid_ref[s] == 1)
    def _():
        x = x_ref[...].astype(F32)
        g = jnp.dot(x, wg_buf[slot], preferred_element_type=F32)
        u = jnp.dot(x, wu_buf[slot], preferred_element_type=F32)
        h_ref[...] = (jax.nn.silu(g) * u).astype(h_ref.dtype)

    @pl.when(valid_ref[s] == 0)
    def _():
        h_ref[...] = jnp.zeros(h_ref.shape, h_ref.dtype)


def _step_maps():
    x_map = lambda s, tile, exp, wchunk, ochunk, *_: (tile[s], 0)
    o_map = lambda s, tile, exp, wchunk, ochunk, *_: (tile[s], ochunk[s])
    return x_map, o_map


def _expert_up(steps, xs, w_gate, w_up, layer, *, tm, tn, name):
    n_rows, d = xs.shape
    d_exp = w_gate.shape[-1]
    x_map, o_map = _step_maps()
    grid_spec = pltpu.PrefetchScalarGridSpec(
        num_scalar_prefetch=len(steps), grid=(steps[0].shape[0],),
        in_specs=[pl.BlockSpec((tm, d), x_map),
                  pl.BlockSpec(memory_space=pl.ANY),
                  pl.BlockSpec(memory_space=pl.ANY)],
        out_specs=pl.BlockSpec((tm, tn), o_map),
        scratch_shapes=[pltpu.VMEM((2, d, tn), w_gate.dtype), pltpu.VMEM((2, d, tn), w_up.dtype),
                        pltpu.SemaphoreType.DMA((2,))])
    return pl.pallas_call(
        functools.partial(_expert_up_body, layer=layer, tn=tn), grid_spec=grid_spec,
        out_shape=jax.ShapeDtypeStruct((n_rows, d_exp), BF16),
        compiler_params=_params(("arbitrary",)), name=name,
    )(*steps, xs, w_gate, w_up)


def _expert_down_body(tile_ref, exp_ref, wchunk_ref, ochunk_ref, valid_ref, grp_ref, first_ref,
                      more_ref, nexp_ref, nchunk_ref, h_ref, wd_hbm, y_ref, wd_buf, sem,
                      *, layer, tn):
    s = pl.program_id(0)

    def copies(e, c, slot):
        cols = pl.ds(pl.multiple_of(c * tn, tn), tn)
        return (pltpu.make_async_copy(wd_hbm.at[layer, e, :, cols], wd_buf.at[slot], sem.at[slot]),)

    slot = _weight_ring(s, exp_ref, wchunk_ref, grp_ref, first_ref, more_ref, nexp_ref,
                        nchunk_ref, copies)

    @pl.when(valid_ref[s] == 1)
    def _():
        y_ref[...] = jnp.dot(h_ref[...].astype(F32), wd_buf[slot],
                             preferred_element_type=F32).astype(y_ref.dtype)

    @pl.when(valid_ref[s] == 0)
    def _():
        y_ref[...] = jnp.zeros(y_ref.shape, y_ref.dtype)


def _expert_down(steps, h, w_down, layer, *, tm, tn, name):
    n_rows, d_exp = h.shape
    d = w_down.shape[-1]
    x_map, o_map = _step_maps()
    grid_spec = pltpu.PrefetchScalarGridSpec(
        num_scalar_prefetch=len(steps), grid=(steps[0].shape[0],),
        in_specs=[pl.BlockSpec((tm, d_exp), x_map),
                  pl.BlockSpec(memory_space=pl.ANY)],
        out_specs=pl.BlockSpec((tm, tn), o_map),
        scratch_shapes=[pltpu.VMEM((2, d_exp, tn), w_down.dtype), pltpu.SemaphoreType.DMA((2,))])
    return pl.pallas_call(
        functools.partial(_expert_down_body, layer=layer, tn=tn), grid_spec=grid_spec,
        out_shape=jax.ShapeDtypeStruct((n_rows, d), F32),
        compiler_params=_params(("arbitrary",)), name=name,
    )(*steps, h, w_down)


def _combine_body(p1_ref, p2_ref, y_hbm, x_ref, g1_ref, g2_ref, lg_ref, lb_ref, o_ref, ob_ref,
                  buf1, buf2, sem, *, tm, alpha):
    step = pl.program_id(0)
    n_steps = pl.num_programs(0)

    def issue(tile, slot):
        base = tile * tm

        def start(i, c):
            _row_copy(y_hbm, p1_ref[base + i], buf1.at[slot], i, sem.at[slot]).start()
            _row_copy(y_hbm, p2_ref[base + i], buf2.at[slot], i, sem.at[slot]).start()
            return c

        lax.fori_loop(0, tm, start, 0, unroll=GATHER_UNROLL)

    @pl.when(step == 0)
    def _():
        issue(0, 0)

    @pl.when(step + 1 < n_steps)
    def _():
        issue(step + 1, (step + 1) % 2)

    slot = step % 2

    def wait(i, c):
        _row_copy(y_hbm, 0, buf1.at[slot], i, sem.at[slot]).wait()
        _row_copy(y_hbm, 0, buf2.at[slot], i, sem.at[slot]).wait()
        return c

    lax.fori_loop(0, tm, wait, 0, unroll=GATHER_UNROLL)
    moe = g1_ref[...] * buf1[slot] + g2_ref[...] * buf2[slot]
    r = _ln(alpha * x_ref[...] + moe, lg_ref[...], lb_ref[...])
    o_ref[...] = r
    ob_ref[...] = r.astype(BF16)


def _combine(pos1, pos2, y, x, g1, g2, ln_g, ln_b, layer, alpha, *, tm, name):
    m, d = x.shape
    row = lambda i, p1, p2: (i, 0)
    grid_spec = pltpu.PrefetchScalarGridSpec(
        num_scalar_prefetch=2, grid=(m // tm,),
        in_specs=[pl.BlockSpec(memory_space=pl.ANY),
                  pl.BlockSpec((tm, d), row),
                  pl.BlockSpec((tm, 1), row),
                  pl.BlockSpec((tm, 1), row),
                  pl.BlockSpec((None, 1, d), lambda i, p1, p2: (layer, 0, 0)),
                  pl.BlockSpec((None, 1, d), lambda i, p1, p2: (layer, 0, 0))],
        out_specs=[pl.BlockSpec((tm, d), row), pl.BlockSpec((tm, d), row)],
        scratch_shapes=[pltpu.VMEM((2, tm, d), y.dtype), pltpu.VMEM((2, tm, d), y.dtype),
                        pltpu.SemaphoreType.DMA((2,))])
    return pl.pallas_call(
        functools.partial(_combine_body, tm=tm, alpha=alpha), grid_spec=grid_spec,
        out_shape=[jax.ShapeDtypeStruct((m, d), F32), jax.ShapeDtypeStruct((m, d), BF16)],
        compiler_params=_params(("arbitrary",)), name=name,
    )(pos1, pos2, y, x, g1, g2, ln_g, ln_b)


def _dispatch_plan(e1, e2, n_experts, tm, n_chunks_up, n_chunks_down):
    t = e1.shape[0]
    e_flat = jnp.concatenate([e1.reshape(t), e2.reshape(t)])
    onehot = (e_flat[:, None] == jnp.arange(n_experts, dtype=jnp.int32)[None, :]).astype(jnp.int32)
    csum = jnp.cumsum(onehot, axis=0)
    counts = csum[-1]
    rank = jnp.take_along_axis(csum, e_flat[:, None], axis=1)[:, 0] - 1
    tiles_e = (counts + tm - 1) // tm
    tile_end = jnp.cumsum(tiles_e)
    tile_start = tile_end - tiles_e
    n_tiles = tile_end[-1]
    n_tiles_max = (2 * t + tm - 1) // tm + n_experts
    n_rows_max = n_tiles_max * tm
    dest = tile_start[e_flat] * tm + rank
    token = jnp.concatenate([jnp.arange(t, dtype=jnp.int32)] * 2)
    row_token = jnp.zeros((n_rows_max,), jnp.int32).at[dest].set(token)
    owner = jnp.where(tiles_e > 0, jnp.arange(n_experts, dtype=jnp.int32), n_experts)
    next_expert = jnp.concatenate([lax.cummin(owner, axis=0, reverse=True)[1:],
                                   jnp.full((1,), n_experts, jnp.int32)])

    def steps(n_chunks):
        n_steps = n_tiles_max * n_chunks
        n_valid = n_tiles * n_chunks
        s = jnp.arange(n_steps, dtype=jnp.int32)
        valid = s < n_valid
        sc = jnp.minimum(s, n_valid - 1)
        exp = jnp.sum(sc[:, None] >= (tile_end * n_chunks)[None, :], axis=1).astype(jnp.int32)
        exp = jnp.minimum(exp, n_experts - 1)
        local = sc - tile_start[exp] * n_chunks
        nt = jnp.maximum(tiles_e[exp], 1)
        wchunk = local // nt
        tile = tile_start[exp] + local % nt
        pad = s - n_valid
        tile = jnp.where(valid, tile, n_tiles + pad // n_chunks)
        ochunk = jnp.where(valid, wchunk, pad % n_chunks)
        key = exp * n_chunks + wchunk
        first = jnp.concatenate([jnp.ones((1,), jnp.bool_), key[1:] != key[:-1]]) & valid
        grp = jnp.cumsum(first.astype(jnp.int32)) - 1
        last_chunk = wchunk == n_chunks - 1
        nexp = jnp.where(last_chunk, next_expert[exp], exp)
        nchunk = jnp.where(last_chunk, 0, wchunk + 1)
        more = first & (nexp < n_experts)
        nexp = jnp.minimum(nexp, n_experts - 1)
        i32 = lambda a: a.astype(jnp.int32)
        return (i32(tile), i32(exp), i32(wchunk), i32(ochunk), i32(valid), i32(grp), i32(first),
                i32(more), i32(nexp), i32(nchunk))

    return row_token, dest[:t], dest[t:], steps(n_chunks_up), steps(n_chunks_down)


def kernel(x_prompt, x_sample, cache_ckv, cache_krope, page_table, w_in, q_norm_g, kv_norm_g, w_qb, w_kvb, w_oa, sgu_ln_g, sgu_ln_b, w_s, b_s, w_ob, w_out, ln1_g, ln1_b, w_router, b_router, w_exp_gate, w_exp_up, w_exp_down, ln2_g, ln2_b):
    batch, seq, d_model = x_prompt.shape
    n_seq, dec_seq, _ = x_sample.shape
    assert dec_seq == 1
    depth = w_in.shape[0]
    q_rank = q_norm_g.shape[1]
    kv_rank = kv_norm_g.shape[1]
    rope_dim = cache_krope.shape[-1]
    n_heads = w_kvb.shape[2]
    nope_dim = w_qb.shape[2] // n_heads - rope_dim
    v_dim = w_kvb.shape[3] - nope_dim
    d_sgu = sgu_ln_g.shape[1]
    n_groups, chunk = w_s.shape[1], w_s.shape[2]
    n_experts = w_router.shape[1]
    d_exp = w_exp_gate.shape[3]
    n_pages = page_table.shape[1]
    past_len = n_pages * cache_ckv.shape[2]
    assert nope_dim == LANES and v_dim == LANES and 2 * rope_dim == LANES
    assert d_sgu == n_groups * LANES and chunk == LANES and seq % chunk == 0
    half = rope_dim // 2
    alpha = float((2 * depth) ** 0.25)
    scale = float((nope_dim + rope_dim) ** -0.5)

    n_prompt = batch * seq
    m = n_prompt + n_seq
    tm = _row_tile(m, ROW_TILE)

    c0 = q_rank + kv_rank
    c1 = c0 + rope_dim
    c2 = c1 + d_sgu
    c3 = c2 + d_sgu
    w_in_t = jnp.swapaxes(w_in, 1, 2)
    w_small = jnp.concatenate(
        [w_in_t[:, :c1], -w_in_t[:, c0 + half:c1], w_in_t[:, c0:c0 + half]], axis=1).astype(BF16)
    w_u = w_in_t[:, c1:c2].astype(BF16)
    w_v = w_in_t[:, c2:c3].astype(BF16)
    w_g = w_in_t[:, c3:].astype(BF16)
    cache_krope_t = jnp.swapaxes(cache_krope, 2, 3)
    wq = w_qb.reshape(depth, q_rank, n_heads, nope_dim + rope_dim)
    wq_r = wq[..., nope_dim:]
    wq_ext = jnp.concatenate(
        [wq, -wq_r[..., half:], wq_r[..., :half]], axis=-1
    ).reshape(depth, q_rank, n_heads * 2 * LANES).astype(BF16)
    wk_flat = w_kvb[..., :nope_dim].reshape(depth, kv_rank, n_heads * LANES).astype(BF16)
    wv_flat = w_kvb[..., nope_dim:].reshape(depth, kv_rank, n_heads * LANES).astype(BF16)
    wk_t = jnp.transpose(w_kvb[..., :nope_dim], (0, 2, 3, 1)).astype(BF16)
    wv_h = jnp.transpose(w_kvb[..., nope_dim:], (0, 2, 1, 3)).astype(BF16)
    w_oa_b = w_oa.astype(BF16)
    w_ob_b = w_ob.astype(BF16)
    w_out_b = w_out.astype(BF16)
    w_s_first = jnp.repeat(w_s[:, :, 0, 0], LANES, axis=1).reshape(depth, 1, d_sgu)
    b_s_first = jnp.repeat(b_s[:, :, 0], LANES, axis=1).reshape(depth, 1, d_sgu)
    b_s_t = jnp.transpose(b_s, (0, 2, 1))
    g3 = lambda a: a.reshape(depth, 1, a.shape[1])
    q_norm3, kv_norm3 = g3(q_norm_g), g3(kv_norm_g)
    sgu_g3, sgu_b3 = g3(sgu_ln_g), g3(sgu_ln_b)
    ln1_g3, ln1_b3, ln2_g3, ln2_b3 = g3(ln1_g), g3(ln1_b), g3(ln2_g), g3(ln2_b)

    pos = jnp.concatenate([jnp.tile(jnp.arange(seq, dtype=jnp.int32), batch),
                           jnp.full((n_seq,), past_len, jnp.int32)])
    inv = ROPE_THETA ** (-jnp.arange(half, dtype=F32) / half)
    ang = pos.astype(F32)[:, None] * inv[None, :]
    cs = jnp.concatenate([jnp.cos(ang), jnp.cos(ang), jnp.sin(ang), jnp.sin(ang)], axis=1)

    x = jnp.concatenate([x_prompt.reshape(n_prompt, d_model), x_sample.reshape(n_seq, d_model)])
    xb = x.astype(BF16)

    n_small = w_small.shape[1]
    tn_q = _col_tile(wq_ext.shape[-1], COL_TILE)
    heads_q = tn_q // (2 * LANES)
    tn_d = _col_tile(d_model, COL_TILE)
    up_chunks = d_exp // _col_tile(d_exp, EXPERT_COL_TILE)
    down_chunks = d_model // _col_tile(d_model, DOWN_COL_TILE)
    cs_spec = (cs, (tm, LANES), lambda i, j: (i, 0))

    ckv_p, kr_p, ckv_d, kr_d, v_d = [], [], [], [], []
    for l in range(depth):
        qn, ckv, kr, kr_pad = _matmul(
            xb, w_small, l, tm=tm, tn=n_small, w_transposed=True,
            epi=functools.partial(_epi_small, q_rank=q_rank, kv_rank=kv_rank),
            extras=[cs_spec,
                    (q_norm3, (None, 1, q_rank), lambda i, j: (l, 0, 0)),
                    (kv_norm3, (None, 1, kv_rank), lambda i, j: (l, 0, 0))],
            outs=[(q_rank, q_rank, BF16), (kv_rank, kv_rank, F32), (rope_dim, rope_dim, F32),
                  (LANES, LANES, BF16)],
            name=f"in_small_{l}")
        (u,) = _matmul(xb, w_u, l, tm=tm, tn=_col_tile(d_sgu, COL_TILE), epi=_epi_gelu,
                       outs=[(d_sgu, _col_tile(d_sgu, COL_TILE), BF16)], w_transposed=True,
                       name=f"in_u_{l}")
        (v_pre,) = _matmul(xb, w_v, l, tm=tm, tn=_col_tile(d_sgu, COL_TILE), epi=_epi_gelu,
                           outs=[(d_sgu, _col_tile(d_sgu, COL_TILE), F32)], w_transposed=True,
                           name=f"in_v_{l}")
        v = _ln_rows(v_pre, sgu_g3, sgu_b3, l, tm=_row_tile(m, 512), name=f"sgu_ln_{l}")
        (gates,) = _matmul(xb, w_g, l, tm=tm, tn=tn_d, epi=_epi_sigmoid,
                           outs=[(2 * d_model, tn_d, BF16)], w_transposed=True,
                           name=f"in_gates_{l}")

        (q_full,) = _matmul(qn, wq_ext, l, tm=tm, tn=tn_q,
                            epi=functools.partial(_epi_q, heads=heads_q, scale=scale),
                            extras=[cs_spec],
                            outs=[(n_heads * 2 * LANES, tn_q, BF16)], name=f"q_proj_{l}")
        k_full, v_full = _kv_expand(ckv, kr_pad, wk_flat, wv_flat, l, n_prompt,
                                    tm=_row_tile(n_prompt, 1024),
                                    heads_per_step=min(n_heads, 8), name=f"kv_expand_{l}")
        o_prompt = _prompt_attn(q_full, k_full, v_full, batch, seq,
                                tq=min(seq, ATTN_Q_TILE), name=f"prompt_attn_{l}")
        q_dec = q_full[n_prompt:]
        q_lat = _headwise(q_dec, wk_t, l, in_stride=2 * LANES,
                          heads_per_step=min(n_heads, 8), name=f"q_absorb_{l}")
        q_rope_dec = q_dec.reshape(n_seq, n_heads, 2 * LANES)[:, :, LANES:]
        o_lat = _paged_attn(page_table, q_lat.reshape(n_seq, n_heads, kv_rank), q_rope_dec,
                            ckv[n_prompt:].reshape(n_seq, 1, kv_rank),
                            kr[n_prompt:].reshape(n_seq, 1, rope_dim),
                            cache_ckv, cache_krope_t, l, name=f"paged_attn_{l}")
        o_dec = _headwise(o_lat.reshape(n_seq, n_heads * kv_rank), wv_h, l, in_stride=kv_rank,
                          heads_per_step=min(n_heads, 8), name=f"v_up_{l}")
        o = jnp.concatenate([o_prompt, o_dec])

        ub_prompt = _sgu_prompt(u, v, w_s, b_s_t, l, n_prompt, name=f"sgu_prompt_{l}")
        ub_dec = _sgu_sample(u[n_prompt:], v[n_prompt:], w_s_first, b_s_first, l,
                             name=f"sgu_sample_{l}")
        ub = jnp.concatenate([ub_prompt, ub_dec])

        h = _gate_merge(o, ub, w_oa_b, w_ob_b, gates, l, tm=tm, tn=_col_tile(d_model, 512),
                        name=f"gate_merge_{l}")
        (mix,) = _matmul(h, w_out_b, l, tm=tm, tn=tn_d, epi=_epi_id,
                         outs=[(d_model, tn_d, F32)], name=f"out_proj_{l}")
        x1, x1b = _ln_residual(x, mix, ln1_g3, ln1_b3, l, alpha, tm=_row_tile(m, 128),
                               name=f"ln1_{l}")

        e1, e2, g1, g2 = _router(x1b, w_router, b_router, tm=tm, name=f"router_{l}")
        row_token, pos1, pos2, steps_up, steps_down = _dispatch_plan(
            e1, e2, n_experts, EXPERT_ROW_TILE, up_chunks, down_chunks)
        xs = _gather_rows(x1, row_token, tm=EXPERT_ROW_TILE, name=f"moe_gather_{l}")
        hid = _expert_up(steps_up, xs, w_exp_gate, w_exp_up, l, tm=EXPERT_ROW_TILE,
                         tn=d_exp // up_chunks, name=f"expert_up_{l}")
        y = _expert_down(steps_down, hid, w_exp_down, l, tm=EXPERT_ROW_TILE,
                         tn=d_model // down_chunks, name=f"expert_down_{l}")
        x, xb = _combine(pos1, pos2, y, x1, g1, g2, ln2_g3, ln2_b3, l, alpha,
                         tm=_row_tile(m, 128), name=f"moe_combine_{l}")

        ckv_p.append(ckv[:n_prompt].reshape(batch, seq, kv_rank))
        kr_p.append(kr[:n_prompt].reshape(batch, seq, rope_dim))
        ckv_d.append(ckv[n_prompt:].reshape(n_seq, 1, kv_rank))
        kr_d.append(kr[n_prompt:].reshape(n_seq, 1, rope_dim))
        v_d.append(v[n_prompt:].reshape(n_seq, 1, d_sgu))

    return (x[:n_prompt].reshape(batch, seq, d_model), x[n_prompt:].reshape(n_seq, 1, d_model),
            jnp.stack(ckv_p), jnp.stack(kr_p), jnp.stack(ckv_d), jnp.stack(kr_d), jnp.stack(v_d))
```

```python
import functools

import jax
import jax.numpy as jnp
from jax import lax
from jax.experimental import pallas as pl
from jax.experimental.pallas import tpu as pltpu

F32 = jnp.float32
BF16 = jnp.bfloat16

ROPE_THETA = 10000.0
LN_EPS = 1e-5
RMS_EPS = 1e-6
N_EXPERT_GROUPS = 4

LANES = 128
SUBLANES = 8
VMEM_LIMIT_BYTES = 52 * 1024 * 1024

ROW_TILE = 640
COL_TILE = 1024
EXPERT_ROW_TILE = 128
EXPERT_COL_TILE = 512
DOWN_COL_TILE = 2048
ATTN_Q_TILE = 512
GATHER_UNROLL = 8
GATHER_SLOTS = 6
COMBINE_SLOTS = 3


def _params(semantics):
    return pltpu.CompilerParams(dimension_semantics=semantics, vmem_limit_bytes=VMEM_LIMIT_BYTES)


def _row_tile(m, target):
    best = None
    for d in range(SUBLANES, min(m, target) + 1, SUBLANES):
        if m % d == 0:
            best = d
    return best if best is not None else m


def _col_tile(n, target):
    best = None
    for d in range(LANES, min(n, target) + 1, LANES):
        if n % d == 0:
            best = d
    return best if best is not None else n


def _mm_body(*refs, n_extra, n_out, epi, w_transposed):
    x_ref, w_ref = refs[0], refs[1]
    extra = refs[2:2 + n_extra]
    outs = refs[2 + n_extra:2 + n_extra + n_out]
    if w_transposed:
        acc = lax.dot_general(x_ref[...], w_ref[...], (((1,), (1,)), ((), ())),
                              preferred_element_type=F32)
    else:
        acc = jnp.dot(x_ref[...], w_ref[...], preferred_element_type=F32)
    res = epi(acc, *[e[...] for e in extra])
    if not isinstance(res, tuple):
        res = (res,)
    for o_ref, r in zip(outs, res):
        o_ref[...] = r.astype(o_ref.dtype)


def _matmul(x, w, layer, *, tm, tn, epi, extras=(), outs, name, w_transposed=False):
    m, k = x.shape
    n = w.shape[1] if w_transposed else w.shape[2]
    grid = (m // tm, n // tn)
    if w_transposed:
        w_spec = pl.BlockSpec((None, tn, k), lambda i, j: (layer, j, 0))
    else:
        w_spec = pl.BlockSpec((None, k, tn), lambda i, j: (layer, 0, j))
    in_specs = [pl.BlockSpec((tm, k), lambda i, j: (i, 0)), w_spec]
    in_specs += [pl.BlockSpec(bs, im) for (_, bs, im) in extras]
    out_specs = [pl.BlockSpec((tm, ct), lambda i, j: (i, j)) for (_, ct, _) in outs]
    out_shape = [jax.ShapeDtypeStruct((m, c), dt) for (c, _, dt) in outs]
    body = functools.partial(_mm_body, n_extra=len(extras), n_out=len(outs), epi=epi,
                             w_transposed=w_transposed)
    res = pl.pallas_call(
        body, grid=grid, in_specs=in_specs, out_specs=out_specs, out_shape=out_shape,
        compiler_params=_params(("parallel", "arbitrary")), name=name,
    )(x, w, *[a for (a, _, _) in extras])
    return res


def _rope_pair(t, cs):
    tt = t * cs
    return tt + pltpu.roll(tt, LANES // 2, 1)


def _epi_small(acc, cs, gq, gkv, *, q_rank, kv_rank):
    qa = acc[:, :q_rank]
    ca = acc[:, q_rank:q_rank + kv_rank]
    t = acc[:, q_rank + kv_rank:]
    qn = qa * lax.rsqrt(jnp.mean(qa * qa, axis=-1, keepdims=True) + RMS_EPS) * gq
    ckv = ca * lax.rsqrt(jnp.mean(ca * ca, axis=-1, keepdims=True) + RMS_EPS) * gkv
    r = _rope_pair(t, cs)
    lane = lax.broadcasted_iota(jnp.int32, r.shape, 1)
    kr_pad = jnp.where(lane < LANES // 2, r, 0.0)
    return qn, ckv, r[:, :LANES // 2], kr_pad


def _epi_q(acc, cs, *, heads, scale):
    parts = []
    for h in range(heads):
        blk = acc[:, h * 2 * LANES:(h + 1) * 2 * LANES]
        parts.append(blk[:, :LANES] * scale)
        parts.append(_rope_pair(blk[:, LANES:], cs) * scale)
    return jnp.concatenate(parts, axis=1)


def _epi_gelu(acc):
    return jax.nn.gelu(acc)


def _epi_sigmoid(acc):
    return jax.nn.sigmoid(acc)


def _epi_id(acc):
    return acc


def _ln(x, g, b):
    mu = jnp.mean(x, axis=-1, keepdims=True)
    xc = x - mu
    var = jnp.mean(xc * xc, axis=-1, keepdims=True)
    return xc * lax.rsqrt(var + LN_EPS) * g + b


def _ln_rows_body(x_ref, g_ref, b_ref, o_ref):
    o_ref[...] = _ln(x_ref[...], g_ref[...], b_ref[...])


def _ln_rows(x, g, b, layer, *, tm, name):
    m, n = x.shape
    return pl.pallas_call(
        _ln_rows_body, grid=(m // tm,),
        in_specs=[pl.BlockSpec((tm, n), lambda i: (i, 0)),
                  pl.BlockSpec((None, 1, n), lambda i: (layer, 0, 0)),
                  pl.BlockSpec((None, 1, n), lambda i: (layer, 0, 0))],
        out_specs=pl.BlockSpec((tm, n), lambda i: (i, 0)),
        out_shape=jax.ShapeDtypeStruct((m, n), F32),
        compiler_params=_params(("parallel",)), name=name,
    )(x, g, b)


def _ln_residual_body(x_ref, y_ref, g_ref, b_ref, o_ref, ob_ref, *, alpha):
    r = _ln(alpha * x_ref[...] + y_ref[...], g_ref[...], b_ref[...])
    o_ref[...] = r
    ob_ref[...] = r.astype(BF16)


def _ln_residual(x, y, g, b, layer, alpha, *, tm, name):
    m, n = x.shape
    return pl.pallas_call(
        functools.partial(_ln_residual_body, alpha=alpha), grid=(m // tm,),
        in_specs=[pl.BlockSpec((tm, n), lambda i: (i, 0)),
                  pl.BlockSpec((tm, n), lambda i: (i, 0)),
                  pl.BlockSpec((None, 1, n), lambda i: (layer, 0, 0)),
                  pl.BlockSpec((None, 1, n), lambda i: (layer, 0, 0))],
        out_specs=[pl.BlockSpec((tm, n), lambda i: (i, 0)),
                   pl.BlockSpec((tm, n), lambda i: (i, 0))],
        out_shape=[jax.ShapeDtypeStruct((m, n), F32), jax.ShapeDtypeStruct((m, n), BF16)],
        compiler_params=_params(("parallel",)), name=name,
    )(x, y, g, b)


def _kv_expand_body(c_ref, kr_ref, wk_ref, wv_ref, k_ref, v_ref, *, heads):
    c = c_ref[...].astype(BF16)
    kn = jnp.dot(c, wk_ref[...], preferred_element_type=F32).astype(BF16)
    kr = kr_ref[...]
    for h in range(heads):
        k_ref[:, h * 2 * LANES:h * 2 * LANES + LANES] = kn[:, h * LANES:(h + 1) * LANES]
        k_ref[:, h * 2 * LANES + LANES:(h + 1) * 2 * LANES] = kr
    v_ref[...] = jnp.dot(c, wv_ref[...], preferred_element_type=F32).astype(BF16)


def _kv_expand(ckv, kr_pad, wk, wv, layer, n_rows, *, tm, heads_per_step, name):
    kv_rank = ckv.shape[1]
    n_heads = wk.shape[-1] // LANES
    hb = heads_per_step
    return pl.pallas_call(
        functools.partial(_kv_expand_body, heads=hb), grid=(n_rows // tm, n_heads // hb),
        in_specs=[pl.BlockSpec((tm, kv_rank), lambda i, j: (i, 0)),
                  pl.BlockSpec((tm, LANES), lambda i, j: (i, 0)),
                  pl.BlockSpec((None, kv_rank, hb * LANES), lambda i, j: (layer, 0, j)),
                  pl.BlockSpec((None, kv_rank, hb * LANES), lambda i, j: (layer, 0, j))],
        out_specs=[pl.BlockSpec((tm, hb * 2 * LANES), lambda i, j: (i, j)),
                   pl.BlockSpec((tm, hb * LANES), lambda i, j: (i, j))],
        out_shape=[jax.ShapeDtypeStruct((n_rows, n_heads * 2 * LANES), BF16),
                   jax.ShapeDtypeStruct((n_rows, n_heads * LANES), BF16)],
        compiler_params=_params(("parallel", "arbitrary")), name=name,
    )(ckv, kr_pad, wk, wv)


def _prompt_attn_body(q_ref, k_ref, v_ref, o_ref, *, tq, seq):
    nt = (((1,), (1,)), ((), ()))
    row = lax.broadcasted_iota(jnp.int32, (tq, tq), 0)
    col = lax.broadcasted_iota(jnp.int32, (tq, tq), 1)
    causal = col <= row
    for qi in range(seq // tq):
        off, kend = qi * tq, (qi + 1) * tq
        q = q_ref[off:kend, :]
        sd = lax.dot_general(q, k_ref[off:kend, :], nt, preferred_element_type=F32)
        sd = jnp.where(causal, sd, -jnp.inf)
        m = jnp.max(sd, axis=-1, keepdims=True)
        if qi > 0:
            so = lax.dot_general(q, k_ref[0:off, :], nt, preferred_element_type=F32)
            m = jnp.maximum(m, jnp.max(so, axis=-1, keepdims=True))
            po = jnp.exp(so - m)
        pd = jnp.exp(sd - m)
        l = jnp.sum(pd, axis=-1, keepdims=True)
        o = jnp.dot(pd.astype(BF16), v_ref[off:kend, :], preferred_element_type=F32)
        if qi > 0:
            l = l + jnp.sum(po, axis=-1, keepdims=True)
            o = o + jnp.dot(po.astype(BF16), v_ref[0:off, :], preferred_element_type=F32)
        o_ref[off:kend, :] = (o / l).astype(o_ref.dtype)


def _prompt_attn(q, k, v, batch, seq, *, tq, name):
    n_heads = v.shape[1] // LANES
    return pl.pallas_call(
        functools.partial(_prompt_attn_body, tq=tq, seq=seq), grid=(batch, n_heads),
        in_specs=[pl.BlockSpec((seq, 2 * LANES), lambda b, h: (b, h)),
                  pl.BlockSpec((seq, 2 * LANES), lambda b, h: (b, h)),
                  pl.BlockSpec((seq, LANES), lambda b, h: (b, h))],
        out_specs=pl.BlockSpec((seq, LANES), lambda b, h: (b, h)),
        out_shape=jax.ShapeDtypeStruct((batch * seq, n_heads * LANES), BF16),
        compiler_params=_params(("parallel", "parallel")), name=name,
    )(q, k, v)


def _headwise_body(x_ref, w_ref, o_ref, *, heads, in_stride, d_in, d_out):
    for h in range(heads):
        xh = x_ref[:, h * in_stride:h * in_stride + d_in]
        o_ref[:, h * d_out:(h + 1) * d_out] = jnp.dot(
            xh, w_ref[h], preferred_element_type=F32).astype(o_ref.dtype)


def _headwise(x, w, layer, *, in_stride, heads_per_step, name):
    m = x.shape[0]
    n_heads, d_in, d_out = w.shape[1:]
    hb = heads_per_step
    return pl.pallas_call(
        functools.partial(_headwise_body, heads=hb, in_stride=in_stride, d_in=d_in, d_out=d_out),
        grid=(n_heads // hb,),
        in_specs=[pl.BlockSpec((m, hb * in_stride), lambda j: (0, j)),
                  pl.BlockSpec((None, hb, d_in, d_out), lambda j: (layer, j, 0, 0))],
        out_specs=pl.BlockSpec((m, hb * d_out), lambda j: (0, j)),
        out_shape=jax.ShapeDtypeStruct((m, n_heads * d_out), BF16),
        compiler_params=_params(("parallel",)), name=name,
    )(x, w)


def _paged_attn_body(pt_ref, ql_ref, qr_ref, cn_ref, rn_ref, c_hbm, r_hbm, o_ref,
                     cbuf, rbuf, sem, *, n_pages, rope_dim, layer):
    b = pl.program_id(0)
    slot = b % 2

    def page_copies(seq, i, s):
        pg = pt_ref[seq, i]
        return (pltpu.make_async_copy(c_hbm.at[layer, pg], cbuf.at[s, i], sem.at[s]),
                pltpu.make_async_copy(r_hbm.at[layer, pg], rbuf.at[s, i], sem.at[s]))

    def issue(seq, s):
        def body(i, c):
            for cp in page_copies(seq, i, s):
                cp.start()
            return c

        lax.fori_loop(0, n_pages, body, 0, unroll=GATHER_UNROLL)

    @pl.when(b == 0)
    def _():
        issue(0, 0)

    def wait(i, c):
        for cp in page_copies(b, i, slot):
            cp.wait()
        return c

    lax.fori_loop(0, n_pages, wait, 0, unroll=GATHER_UNROLL)

    @pl.when(b + 1 < pl.num_programs(0))
    def _():
        issue(b + 1, 1 - slot)

    ql = ql_ref[...]
    qr = qr_ref[...][:, :rope_dim]
    nt = (((1,), (1,)), ((), ()))
    ss = []
    for i in range(n_pages):
        c = cbuf[slot, i].astype(BF16)
        r_t = rbuf[slot, i].astype(BF16)
        ss.append(lax.dot_general(ql, c, nt, preferred_element_type=F32)
                  + jnp.dot(qr, r_t, preferred_element_type=F32))
    s = jnp.concatenate(ss, axis=1)
    cn = cn_ref[...]
    rn = rn_ref[...]
    s_self = (jnp.sum(ql.astype(F32) * cn, axis=-1, keepdims=True)
              + jnp.sum(qr.astype(F32) * rn, axis=-1, keepdims=True))
    m = jnp.maximum(jnp.max(s, axis=-1, keepdims=True), s_self)
    p = jnp.exp(s - m)
    p_self = jnp.exp(s_self - m)
    l = jnp.sum(p, axis=-1, keepdims=True) + p_self
    pv = p_self * cn
    for i in range(n_pages):
        pv = pv + jnp.dot(p[:, i * LANES:(i + 1) * LANES].astype(BF16),
                          cbuf[slot, i].astype(BF16), preferred_element_type=F32)
    o_ref[...] = (pv / l).astype(o_ref.dtype)


def _paged_attn(page_table, q_lat, q_rope, c_new, r_new, cache_c, cache_r_t, layer, *, name):
    n_seq, n_heads, kv_rank = q_lat.shape
    n_pages = page_table.shape[1]
    page = cache_c.shape[2]
    rope_dim = cache_r_t.shape[2]
    assert page == LANES
    seq_map = lambda b, pt: (b, 0, 0)
    grid_spec = pltpu.PrefetchScalarGridSpec(
        num_scalar_prefetch=1, grid=(n_seq,),
        in_specs=[pl.BlockSpec((None, n_heads, kv_rank), seq_map),
                  pl.BlockSpec((None, n_heads, LANES), seq_map),
                  pl.BlockSpec((None, 1, kv_rank), seq_map),
                  pl.BlockSpec((None, 1, rope_dim), seq_map),
                  pl.BlockSpec(memory_space=pl.ANY),
                  pl.BlockSpec(memory_space=pl.ANY)],
        out_specs=pl.BlockSpec((None, n_heads, kv_rank), seq_map),
        scratch_shapes=[pltpu.VMEM((2, n_pages, page, kv_rank), cache_c.dtype),
                        pltpu.VMEM((2, n_pages, rope_dim, page), cache_r_t.dtype),
                        pltpu.SemaphoreType.DMA((2,))])
    return pl.pallas_call(
        functools.partial(_paged_attn_body, n_pages=n_pages, rope_dim=rope_dim, layer=layer),
        grid_spec=grid_spec,
        out_shape=jax.ShapeDtypeStruct((n_seq, n_heads, kv_rank), BF16),
        compiler_params=_params(("arbitrary",)), name=name,
    )(page_table, q_lat, q_rope, c_new, r_new, cache_c, cache_r_t)


def _sgu_prompt_body(u_ref, v_ref, w_ref, bt_ref, o_ref, *, groups):
    chunk = w_ref.shape[-1]
    row = lax.broadcasted_iota(jnp.int32, (chunk, chunk), 0)
    col = lax.broadcasted_iota(jnp.int32, (chunk, chunk), 1)
    tril = col <= row
    for g in range(groups):
        w = jnp.where(tril, w_ref[g], 0.0).astype(BF16)
        vg = v_ref[:, g * LANES:(g + 1) * LANES].astype(BF16)
        mix = jnp.dot(w, vg, preferred_element_type=F32) + bt_ref[:, g:g + 1]
        ug = u_ref[:, g * LANES:(g + 1) * LANES].astype(F32)
        o_ref[:, g * LANES:(g + 1) * LANES] = (ug * mix).astype(o_ref.dtype)


def _sgu_prompt(u, v, w_s, b_s_t, layer, n_rows, *, name):
    groups, chunk = w_s.shape[1], w_s.shape[2]
    d_sgu = u.shape[1]
    return pl.pallas_call(
        functools.partial(_sgu_prompt_body, groups=groups), grid=(n_rows // chunk,),
        in_specs=[pl.BlockSpec((chunk, d_sgu), lambda i: (i, 0)),
                  pl.BlockSpec((chunk, d_sgu), lambda i: (i, 0)),
                  pl.BlockSpec((None, groups, chunk, chunk), lambda i: (layer, 0, 0, 0)),
                  pl.BlockSpec((None, chunk, groups), lambda i: (layer, 0, 0))],
        out_specs=pl.BlockSpec((chunk, d_sgu), lambda i: (i, 0)),
        out_shape=jax.ShapeDtypeStruct((n_rows, d_sgu), BF16),
        compiler_params=_params(("parallel",)), name=name,
    )(u, v, w_s, b_s_t)


def _sgu_sample_body(u_ref, v_ref, w_ref, b_ref, o_ref):
    mix = v_ref[...] * w_ref[...] + b_ref[...]
    o_ref[...] = (u_ref[...].astype(F32) * mix).astype(o_ref.dtype)


def _sgu_sample(u, v, w_row, b_row, layer, *, name):
    m, d_sgu = u.shape
    return pl.pallas_call(
        _sgu_sample_body, grid=(1,),
        in_specs=[pl.BlockSpec((m, d_sgu), lambda i: (0, 0)),
                  pl.BlockSpec((m, d_sgu), lambda i: (0, 0)),
                  pl.BlockSpec((None, 1, d_sgu), lambda i: (layer, 0, 0)),
                  pl.BlockSpec((None, 1, d_sgu), lambda i: (layer, 0, 0))],
        out_specs=pl.BlockSpec((m, d_sgu), lambda i: (0, 0)),
        out_shape=jax.ShapeDtypeStruct((m, d_sgu), BF16),
        compiler_params=_params(("arbitrary",)), name=name,
    )(u, v, w_row, b_row)


def _gate_merge_body(o_ref, ub_ref, wa_ref, wb_ref, ga_ref, gb_ref, h_ref):
    ya = jnp.dot(o_ref[...], wa_ref[...], preferred_element_type=F32)
    yb = jnp.dot(ub_ref[...], wb_ref[...], preferred_element_type=F32)
    h_ref[...] = (ga_ref[...].astype(F32) * ya + gb_ref[...].astype(F32) * yb).astype(h_ref.dtype)


def _gate_merge(o, ub, w_oa, w_ob, gates, layer, *, tm, tn, name):
    m, ka = o.shape
    kb = ub.shape[1]
    n = w_oa.shape[-1]
    nj = n // tn
    return pl.pallas_call(
        _gate_merge_body, grid=(m // tm, nj),
        in_specs=[pl.BlockSpec((tm, ka), lambda i, j: (i, 0)),
                  pl.BlockSpec((tm, kb), lambda i, j: (i, 0)),
                  pl.BlockSpec((None, ka, tn), lambda i, j: (layer, 0, j)),
                  pl.BlockSpec((None, kb, tn), lambda i, j: (layer, 0, j)),
                  pl.BlockSpec((tm, tn), lambda i, j: (i, j)),
                  pl.BlockSpec((tm, tn), lambda i, j: (i, j + nj))],
        out_specs=pl.BlockSpec((tm, tn), lambda i, j: (i, j)),
        out_shape=jax.ShapeDtypeStruct((m, n), BF16),
        compiler_params=_params(("parallel", "arbitrary")), name=name,
    )(o, ub, w_oa, w_ob, gates, gates)


def _router_body(x_ref, w_ref, b_ref, e1_ref, e2_ref, g1_ref, g2_ref, *, n_experts):
    logits = jnp.dot(x_ref[...], w_ref[...].astype(BF16), preferred_element_type=F32)
    scores = jax.nn.sigmoid(logits)
    sel = scores + b_ref[...]
    per_group = n_experts // N_EXPERT_GROUPS
    lane = lax.broadcasted_iota(jnp.int32, sel.shape, 1).astype(F32)
    neg = -jnp.inf

    def top2(vals):
        m1 = jnp.max(vals, axis=-1, keepdims=True)
        i1 = jnp.min(jnp.where(vals == m1, lane, float(n_experts)), axis=-1, keepdims=True)
        rest = jnp.where(lane == i1, neg, vals)
        m2 = jnp.max(rest, axis=-1, keepdims=True)
        i2 = jnp.min(jnp.where(rest == m2, lane, float(n_experts)), axis=-1, keepdims=True)
        return m1, i1, m2, i2

    best = None
    best_g = None
    for g in range(N_EXPERT_GROUPS):
        in_g = (lane >= g * per_group) & (lane < (g + 1) * per_group)
        m1, _, m2, _ = top2(jnp.where(in_g, sel, neg))
        val = m1 + m2
        if g == 0:
            best, best_g = val, jnp.zeros(val.shape, F32)
        else:
            upd = val > best
            best_g = jnp.where(upd, float(g), best_g)
            best = jnp.where(upd, val, best)
    lo = best_g * per_group
    in_best = (lane >= lo) & (lane < lo + per_group)
    _, e1, _, e2 = top2(jnp.where(in_best, sel, neg))
    w1 = jnp.sum(jnp.where(lane == e1, scores, 0.0), axis=-1, keepdims=True)
    w2 = jnp.sum(jnp.where(lane == e2, scores, 0.0), axis=-1, keepdims=True)
    tot = w1 + w2
    e1_ref[...] = e1.astype(jnp.int32)
    e2_ref[...] = e2.astype(jnp.int32)
    g1_ref[...] = w1 / tot
    g2_ref[...] = w2 / tot


def _router(xb, w_router, b_router, *, tm, name):
    m, d = xb.shape
    n_experts = w_router.shape[1]
    col = pl.BlockSpec((tm, 1), lambda i: (i, 0))
    return pl.pallas_call(
        functools.partial(_router_body, n_experts=n_experts), grid=(m // tm,),
        in_specs=[pl.BlockSpec((tm, d), lambda i: (i, 0)),
                  pl.BlockSpec((d, n_experts), lambda i: (0, 0)),
                  pl.BlockSpec((1, n_experts), lambda i: (0, 0))],
        out_specs=[col, col, col, col],
        out_shape=[jax.ShapeDtypeStruct((m, 1), jnp.int32), jax.ShapeDtypeStruct((m, 1), jnp.int32),
                   jax.ShapeDtypeStruct((m, 1), F32), jax.ShapeDtypeStruct((m, 1), F32)],
        compiler_params=_params(("parallel",)), name=name,
    )(xb, w_router, b_router.reshape(1, n_experts))


def _row_copy(src_hbm, row, dst, i, sem):
    return pltpu.make_async_copy(src_hbm.at[pl.ds(row, 1)], dst.at[pl.ds(i, 1)], sem)


def _tile_ring(step, n_steps, slots, issue):
    ahead = slots - 1

    @pl.when(step == 0)
    def _():
        for t in range(min(ahead, n_steps)):
            issue(t, t)

    @pl.when(step + ahead < n_steps)
    def _():
        issue(step + ahead, lax.rem(step + ahead, slots))

    return lax.rem(step, slots)


def _gather_rows_body(tok_ref, x_hbm, o_ref, buf, sem, *, tm, n_steps, slots):
    step = pl.program_id(0)

    def issue(tile, slot):
        base = tile * tm

        def start(i, c):
            _row_copy(x_hbm, tok_ref[base + i], buf.at[slot], i, sem.at[slot]).start()
            return c

        lax.fori_loop(0, tm, start, 0, unroll=GATHER_UNROLL)

    slot = _tile_ring(step, n_steps, slots, issue)

    def wait(i, c):
        _row_copy(x_hbm, 0, buf.at[slot], i, sem.at[slot]).wait()
        return c

    lax.fori_loop(0, tm, wait, 0, unroll=GATHER_UNROLL)
    o_ref[...] = buf[slot].astype(o_ref.dtype)


def _gather_rows(x, row_token, *, tm, name):
    d = x.shape[1]
    n_rows = row_token.shape[0]
    n_steps = n_rows // tm
    slots = GATHER_SLOTS
    grid_spec = pltpu.PrefetchScalarGridSpec(
        num_scalar_prefetch=1, grid=(n_steps,),
        in_specs=[pl.BlockSpec(memory_space=pl.ANY)],
        out_specs=pl.BlockSpec((tm, d), lambda i, tok: (i, 0)),
        scratch_shapes=[pltpu.VMEM((slots, tm, d), x.dtype), pltpu.SemaphoreType.DMA((slots,))])
    return pl.pallas_call(
        functools.partial(_gather_rows_body, tm=tm, n_steps=n_steps, slots=slots),
        grid_spec=grid_spec,
        out_shape=jax.ShapeDtypeStruct((n_rows, d), BF16),
        compiler_params=_params(("arbitrary",)), name=name,
    )(row_token, x)


def _weight_ring(s, exp_ref, wchunk_ref, grp_ref, first_ref, more_ref, nexp_ref, nchunk_ref,
                 copies):
    slot = grp_ref[s] % 2

    @pl.when(s == 0)
    def _():
        for cp in copies(exp_ref[0], wchunk_ref[0], 0):
            cp.start()

    @pl.when(first_ref[s] == 1)
    def _():
        for cp in copies(exp_ref[s], wchunk_ref[s], slot):
            cp.wait()

        @pl.when(more_ref[s] == 1)
        def _():
            for cp in copies(nexp_ref[s], nchunk_ref[s], 1 - slot):
                cp.start()

    return slot


def _expert_up_body(tile_ref, exp_ref, wchunk_ref, ochunk_ref, valid_ref, grp_ref, first_ref,
                    more_ref, nexp_ref, nchunk_ref, x_ref, wg_hbm, wu_hbm, h_ref,
                    wg_buf, wu_buf, sem, *, layer, tn):
    s = pl.program_id(0)

    def copies(e, c, slot):
        cols = pl.ds(pl.multiple_of(c * tn, tn), tn)
        return (pltpu.make_async_copy(wg_hbm.at[layer, e, :, cols], wg_buf.at[slot], sem.at[slot]),
                pltpu.make_async_copy(wu_hbm.at[layer, e, :, cols], wu_buf.at[slot], sem.at[slot]))

    slot = _weight_ring(s, exp_ref, wchunk_ref, grp_ref, first_ref, more_ref, nexp_ref,
                        nchunk_ref, copies)

    @pl.when(valid_ref[s] == 1)
    def _():
        x = x_ref[...].astype(F32)
        g = jnp.dot(x, wg_buf[slot], preferred_element_type=F32)
        u = jnp.dot(x, wu_buf[slot], preferred_element_type=F32)
        h_ref[...] = (jax.nn.silu(g) * u).astype(h_ref.dtype)

    @pl.when(valid_ref[s] == 0)
    def _():
        h_ref[...] = jnp.zeros(h_ref.shape, h_ref.dtype)


def _step_maps():
    x_map = lambda s, tile, exp, wchunk, ochunk, *_: (tile[s], 0)
    o_map = lambda s, tile, exp, wchunk, ochunk, *_: (tile[s], ochunk[s])
    return x_map, o_map


def _expert_up(steps, xs, w_gate, w_up, layer, *, tm, tn, name):
    n_rows, d = xs.shape
    d_exp = w_gate.shape[-1]
    x_map, o_map = _step_maps()
    grid_spec = pltpu.PrefetchScalarGridSpec(
        num_scalar_prefetch=len(steps), grid=(steps[0].shape[0],),
        in_specs=[pl.BlockSpec((tm, d), x_map),
                  pl.BlockSpec(memory_space=pl.ANY),
                  pl.BlockSpec(memory_space=pl.ANY)],
        out_specs=pl.BlockSpec((tm, tn), o_map),
        scratch_shapes=[pltpu.VMEM((2, d, tn), w_gate.dtype), pltpu.VMEM((2, d, tn), w_up.dtype),
                        pltpu.SemaphoreType.DMA((2,))])
    return pl.pallas_call(
        functools.partial(_expert_up_body, layer=layer, tn=tn), grid_spec=grid_spec,
        out_shape=jax.ShapeDtypeStruct((n_rows, d_exp), BF16),
        compiler_params=_params(("arbitrary",)), name=name,
    )(*steps, xs, w_gate, w_up)


def _expert_down_body(tile_ref, exp_ref, wchunk_ref, ochunk_ref, valid_ref, grp_ref, first_ref,
                      more_ref, nexp_ref, nchunk_ref, h_ref, wd_hbm, y_ref, wd_buf, sem,
                      *, layer, tn):
    s = pl.program_id(0)

    def copies(e, c, slot):
        cols = pl.ds(pl.multiple_of(c * tn, tn), tn)
        return (pltpu.make_async_copy(wd_hbm.at[layer, e, :, cols], wd_buf.at[slot], sem.at[slot]),)

    slot = _weight_ring(s, exp_ref, wchunk_ref, grp_ref, first_ref, more_ref, nexp_ref,
                        nchunk_ref, copies)

    @pl.when(valid_ref[s] == 1)
    def _():
        y_ref[...] = jnp.dot(h_ref[...].astype(F32), wd_buf[slot],
                             preferred_element_type=F32).astype(y_ref.dtype)

    @pl.when(valid_ref[s] == 0)
    def _():
        y_ref[...] = jnp.zeros(y_ref.shape, y_ref.dtype)


def _expert_down(steps, h, w_down, layer, *, tm, tn, name):
    n_rows, d_exp = h.shape
    d = w_down.shape[-1]
    x_map, o_map = _step_maps()
    grid_spec = pltpu.PrefetchScalarGridSpec(
        num_scalar_prefetch=len(steps), grid=(steps[0].shape[0],),
        in_specs=[pl.BlockSpec((tm, d_exp), x_map),
                  pl.BlockSpec(memory_space=pl.ANY)],
        out_specs=pl.BlockSpec((tm, tn), o_map),
        scratch_shapes=[pltpu.VMEM((2, d_exp, tn), w_down.dtype), pltpu.SemaphoreType.DMA((2,))])
    return pl.pallas_call(
        functools.partial(_expert_down_body, layer=layer, tn=tn), grid_spec=grid_spec,
        out_shape=jax.ShapeDtypeStruct((n_rows, d), F32),
        compiler_params=_params(("arbitrary",)), name=name,
    )(*steps, h, w_down)


def _combine_body(p1_ref, p2_ref, y_hbm, x_ref, g1_ref, g2_ref, lg_ref, lb_ref, o_ref, ob_ref,
                  buf1, buf2, sem, *, tm, alpha, n_steps, slots):
    step = pl.program_id(0)

    def issue(tile, slot):
        base = tile * tm

        def start(i, c):
            _row_copy(y_hbm, p1_ref[base + i], buf1.at[slot], i, sem.at[slot]).start()
            _row_copy(y_hbm, p2_ref[base + i], buf2.at[slot], i, sem.at[slot]).start()
            return c

        lax.fori_loop(0, tm, start, 0, unroll=GATHER_UNROLL)

    slot = _tile_ring(step, n_steps, slots, issue)

    def wait(i, c):
        _row_copy(y_hbm, 0, buf1.at[slot], i, sem.at[slot]).wait()
        _row_copy(y_hbm, 0, buf2.at[slot], i, sem.at[slot]).wait()
        return c

    lax.fori_loop(0, tm, wait, 0, unroll=GATHER_UNROLL)
    moe = g1_ref[...] * buf1[slot] + g2_ref[...] * buf2[slot]
    r = _ln(alpha * x_ref[...] + moe, lg_ref[...], lb_ref[...])
    o_ref[...] = r
    ob_ref[...] = r.astype(BF16)


def _combine(pos1, pos2, y, x, g1, g2, ln_g, ln_b, layer, alpha, *, tm, name):
    m, d = x.shape
    row = lambda i, p1, p2: (i, 0)
    n_steps = m // tm
    slots = COMBINE_SLOTS
    grid_spec = pltpu.PrefetchScalarGridSpec(
        num_scalar_prefetch=2, grid=(n_steps,),
        in_specs=[pl.BlockSpec(memory_space=pl.ANY),
                  pl.BlockSpec((tm, d), row),
                  pl.BlockSpec((tm, 1), row),
                  pl.BlockSpec((tm, 1), row),
                  pl.BlockSpec((None, 1, d), lambda i, p1, p2: (layer, 0, 0)),
                  pl.BlockSpec((None, 1, d), lambda i, p1, p2: (layer, 0, 0))],
        out_specs=[pl.BlockSpec((tm, d), row), pl.BlockSpec((tm, d), row)],
        scratch_shapes=[pltpu.VMEM((slots, tm, d), y.dtype), pltpu.VMEM((slots, tm, d), y.dtype),
                        pltpu.SemaphoreType.DMA((slots,))])
    return pl.pallas_call(
        functools.partial(_combine_body, tm=tm, alpha=alpha, n_steps=n_steps, slots=slots),
        grid_spec=grid_spec,
        out_shape=[jax.ShapeDtypeStruct((m, d), F32), jax.ShapeDtypeStruct((m, d), BF16)],
        compiler_params=_params(("arbitrary",)), name=name,
    )(pos1, pos2, y, x, g1, g2, ln_g, ln_b)


def _dispatch_plan(e1, e2, n_experts, tm, n_chunks_up, n_chunks_down):
    t = e1.shape[0]
    e_flat = jnp.concatenate([e1.reshape(t), e2.reshape(t)])
    onehot = (e_flat[:, None] == jnp.arange(n_experts, dtype=jnp.int32)[None, :]).astype(jnp.int32)
    csum = jnp.cumsum(onehot, axis=0)
    counts = csum[-1]
    rank = jnp.take_along_axis(csum, e_flat[:, None], axis=1)[:, 0] - 1
    tiles_e = (counts + tm - 1) // tm
    tile_end = jnp.cumsum(tiles_e)
    tile_start = tile_end - tiles_e
    n_tiles = tile_end[-1]
    n_tiles_max = (2 * t + tm - 1) // tm + n_experts
    n_rows_max = n_tiles_max * tm
    dest = tile_start[e_flat] * tm + rank
    token = jnp.concatenate([jnp.arange(t, dtype=jnp.int32)] * 2)
    row_token = jnp.zeros((n_rows_max,), jnp.int32).at[dest].set(token)
    owner = jnp.where(tiles_e > 0, jnp.arange(n_experts, dtype=jnp.int32), n_experts)
    next_expert = jnp.concatenate([lax.cummin(owner, axis=0, reverse=True)[1:],
                                   jnp.full((1,), n_experts, jnp.int32)])

    def steps(n_chunks):
        n_steps = n_tiles_max * n_chunks
        n_valid = n_tiles * n_chunks
        s = jnp.arange(n_steps, dtype=jnp.int32)
        valid = s < n_valid
        sc = jnp.minimum(s, n_valid - 1)
        exp = jnp.sum(sc[:, None] >= (tile_end * n_chunks)[None, :], axis=1).astype(jnp.int32)
        exp = jnp.minimum(exp, n_experts - 1)
        local = sc - tile_start[exp] * n_chunks
        nt = jnp.maximum(tiles_e[exp], 1)
        wchunk = local // nt
        tile = tile_start[exp] + local % nt
        pad = s - n_valid
        tile = jnp.where(valid, tile, n_tiles + pad // n_chunks)
        ochunk = jnp.where(valid, wchunk, pad % n_chunks)
        key = exp * n_chunks + wchunk
        first = jnp.concatenate([jnp.ones((1,), jnp.bool_), key[1:] != key[:-1]]) & valid
        grp = jnp.cumsum(first.astype(jnp.int32)) - 1
        last_chunk = wchunk == n_chunks - 1
        nexp = jnp.where(last_chunk, next_expert[exp], exp)
        nchunk = jnp.where(last_chunk, 0, wchunk + 1)
        more = first & (nexp < n_experts)
        nexp = jnp.minimum(nexp, n_experts - 1)
        i32 = lambda a: a.astype(jnp.int32)
        return (i32(tile), i32(exp), i32(wchunk), i32(ochunk), i32(valid), i32(grp), i32(first),
                i32(more), i32(nexp), i32(nchunk))

    return row_token, dest[:t], dest[t:], steps(n_chunks_up), steps(n_chunks_down)


def kernel(x_prompt, x_sample, cache_ckv, cache_krope, page_table, w_in, q_norm_g, kv_norm_g, w_qb, w_kvb, w_oa, sgu_ln_g, sgu_ln_b, w_s, b_s, w_ob, w_out, ln1_g, ln1_b, w_router, b_router, w_exp_gate, w_exp_up, w_exp_down, ln2_g, ln2_b):
    batch, seq, d_model = x_prompt.shape
    n_seq, dec_seq, _ = x_sample.shape
    assert dec_seq == 1
    depth = w_in.shape[0]
    q_rank = q_norm_g.shape[1]
    kv_rank = kv_norm_g.shape[1]
    rope_dim = cache_krope.shape[-1]
    n_heads = w_kvb.shape[2]
    nope_dim = w_qb.shape[2] // n_heads - rope_dim
    v_dim = w_kvb.shape[3] - nope_dim
    d_sgu = sgu_ln_g.shape[1]
    n_groups, chunk = w_s.shape[1], w_s.shape[2]
    n_experts = w_router.shape[1]
    d_exp = w_exp_gate.shape[3]
    n_pages = page_table.shape[1]
    past_len = n_pages * cache_ckv.shape[2]
    assert nope_dim == LANES and v_dim == LANES and 2 * rope_dim == LANES
    assert d_sgu == n_groups * LANES and chunk == LANES and seq % chunk == 0
    half = rope_dim // 2
    alpha = float((2 * depth) ** 0.25)
    scale = float((nope_dim + rope_dim) ** -0.5)

    n_prompt = batch * seq
    m = n_prompt + n_seq
    tm = _row_tile(m, ROW_TILE)

    c0 = q_rank + kv_rank
    c1 = c0 + rope_dim
    c2 = c1 + d_sgu
    c3 = c2 + d_sgu
    w_in_t = jnp.swapaxes(w_in, 1, 2)
    w_small = jnp.concatenate(
        [w_in_t[:, :c1], -w_in_t[:, c0 + half:c1], w_in_t[:, c0:c0 + half]], axis=1).astype(BF16)
    w_u = w_in_t[:, c1:c2].astype(BF16)
    w_v = w_in_t[:, c2:c3].astype(BF16)
    w_g = w_in_t[:, c3:].astype(BF16)
    cache_krope_t = jnp.swapaxes(cache_krope, 2, 3)
    wq = w_qb.reshape(depth, q_rank, n_heads, nope_dim + rope_dim)
    wq_r = wq[..., nope_dim:]
    wq_ext = jnp.concatenate(
        [wq, -wq_r[..., half:], wq_r[..., :half]], axis=-1
    ).reshape(depth, q_rank, n_heads * 2 * LANES).astype(BF16)
    wk_flat = w_kvb[..., :nope_dim].reshape(depth, kv_rank, n_heads * LANES).astype(BF16)
    wv_flat = w_kvb[..., nope_dim:].reshape(depth, kv_rank, n_heads * LANES).astype(BF16)
    wk_t = jnp.transpose(w_kvb[..., :nope_dim], (0, 2, 3, 1)).astype(BF16)
    wv_h = jnp.transpose(w_kvb[..., nope_dim:], (0, 2, 1, 3)).astype(BF16)
    w_oa_b = w_oa.astype(BF16)
    w_ob_b = w_ob.astype(BF16)
    w_out_b = w_out.astype(BF16)
    w_s_first = jnp.repeat(w_s[:, :, 0, 0], LANES, axis=1).reshape(depth, 1, d_sgu)
    b_s_first = jnp.repeat(b_s[:, :, 0], LANES, axis=1).reshape(depth, 1, d_sgu)
    b_s_t = jnp.transpose(b_s, (0, 2, 1))
    g3 = lambda a: a.reshape(depth, 1, a.shape[1])
    q_norm3, kv_norm3 = g3(q_norm_g), g3(kv_norm_g)
    sgu_g3, sgu_b3 = g3(sgu_ln_g), g3(sgu_ln_b)
    ln1_g3, ln1_b3, ln2_g3, ln2_b3 = g3(ln1_g), g3(ln1_b), g3(ln2_g), g3(ln2_b)

    pos = jnp.concatenate([jnp.tile(jnp.arange(seq, dtype=jnp.int32), batch),
                           jnp.full((n_seq,), past_len, jnp.int32)])
    inv = ROPE_THETA ** (-jnp.arange(half, dtype=F32) / half)
    ang = pos.astype(F32)[:, None] * inv[None, :]
    cs = jnp.concatenate([jnp.cos(ang), jnp.cos(ang), jnp.sin(ang), jnp.sin(ang)], axis=1)

    x = jnp.concatenate([x_prompt.reshape(n_prompt, d_model), x_sample.reshape(n_seq, d_model)])
    xb = x.astype(BF16)

    n_small = w_small.shape[1]
    tn_q = _col_tile(wq_ext.shape[-1], COL_TILE)
    heads_q = tn_q // (2 * LANES)
    tn_d = _col_tile(d_model, COL_TILE)
    up_chunks = d_exp // _col_tile(d_exp, EXPERT_COL_TILE)
    down_chunks = d_model // _col_tile(d_model, DOWN_COL_TILE)
    cs_spec = (cs, (tm, LANES), lambda i, j: (i, 0))

    ckv_p, kr_p, ckv_d, kr_d, v_d = [], [], [], [], []
    for l in range(depth):
        qn, ckv, kr, kr_pad = _matmul(
            xb, w_small, l, tm=tm, tn=n_small, w_transposed=True,
            epi=functools.partial(_epi_small, q_rank=q_rank, kv_rank=kv_rank),
            extras=[cs_spec,
                    (q_norm3, (None, 1, q_rank), lambda i, j: (l, 0, 0)),
                    (kv_norm3, (None, 1, kv_rank), lambda i, j: (l, 0, 0))],
            outs=[(q_rank, q_rank, BF16), (kv_rank, kv_rank, F32), (rope_dim, rope_dim, F32),
                  (LANES, LANES, BF16)],
            name=f"in_small_{l}")
        (u,) = _matmul(xb, w_u, l, tm=tm, tn=_col_tile(d_sgu, COL_TILE), epi=_epi_gelu,
                       outs=[(d_sgu, _col_tile(d_sgu, COL_TILE), BF16)], w_transposed=True,
                       name=f"in_u_{l}")
        (v_pre,) = _matmul(xb, w_v, l, tm=tm, tn=_col_tile(d_sgu, COL_TILE), epi=_epi_gelu,
                           outs=[(d_sgu, _col_tile(d_sgu, COL_TILE), F32)], w_transposed=True,
                           name=f"in_v_{l}")
        v = _ln_rows(v_pre, sgu_g3, sgu_b3, l, tm=_row_tile(m, 512), name=f"sgu_ln_{l}")
        (gates,) = _matmul(xb, w_g, l, tm=tm, tn=tn_d, epi=_epi_sigmoid,
                           outs=[(2 * d_model, tn_d, BF16)], w_transposed=True,
                           name=f"in_gates_{l}")

        (q_full,) = _matmul(qn, wq_ext, l, tm=tm, tn=tn_q,
                            epi=functools.partial(_epi_q, heads=heads_q, scale=scale),
                            extras=[cs_spec],
                            outs=[(n_heads * 2 * LANES, tn_q, BF16)], name=f"q_proj_{l}")
        k_full, v_full = _kv_expand(ckv, kr_pad, wk_flat, wv_flat, l, n_prompt,
                                    tm=_row_tile(n_prompt, 1024),
                                    heads_per_step=min(n_heads, 8), name=f"kv_expand_{l}")
        o_prompt = _prompt_attn(q_full, k_full, v_full, batch, seq,
                                tq=min(seq, ATTN_Q_TILE), name=f"prompt_attn_{l}")
        q_dec = q_full[n_prompt:]
        q_lat = _headwise(q_dec, wk_t, l, in_stride=2 * LANES,
                          heads_per_step=min(n_heads, 8), name=f"q_absorb_{l}")
        q_rope_dec = q_dec.reshape(n_seq, n_heads, 2 * LANES)[:, :, LANES:]
        o_lat = _paged_attn(page_table, q_lat.reshape(n_seq, n_heads, kv_rank), q_rope_dec,
                            ckv[n_prompt:].reshape(n_seq, 1, kv_rank),
                            kr[n_prompt:].reshape(n_seq, 1, rope_dim),
                            cache_ckv, cache_krope_t, l, name=f"paged_attn_{l}")
        o_dec = _headwise(o_lat.reshape(n_seq, n_heads * kv_rank), wv_h, l, in_stride=kv_rank,
                          heads_per_step=min(n_heads, 8), name=f"v_up_{l}")
        o = jnp.concatenate([o_prompt, o_dec])

        ub_prompt = _sgu_prompt(u, v, w_s, b_s_t, l, n_prompt, name=f"sgu_prompt_{l}")
        ub_dec = _sgu_sample(u[n_prompt:], v[n_prompt:], w_s_first, b_s_first, l,
                             name=f"sgu_sample_{l}")
        ub = jnp.concatenate([ub_prompt, ub_dec])

        h = _gate_merge(o, ub, w_oa_b, w_ob_b, gates, l, tm=tm, tn=_col_tile(d_model, 512),
                        name=f"gate_merge_{l}")
        (mix,) = _matmul(h, w_out_b, l, tm=tm, tn=tn_d, epi=_epi_id,
                         outs=[(d_model, tn_d, F32)], name=f"out_proj_{l}")
        x1, x1b = _ln_residual(x, mix, ln1_g3, ln1_b3, l, alpha, tm=_row_tile(m, 128),
                               name=f"ln1_{l}")

        e1, e2, g1, g2 = _router(x1b, w_router, b_router, tm=tm, name=f"router_{l}")
        row_token, pos1, pos2, steps_up, steps_down = _dispatch_plan(
            e1, e2, n_experts, EXPERT_ROW_TILE, up_chunks, down_chunks)
        xs = _gather_rows(x1, row_token, tm=EXPERT_ROW_TILE, name=f"moe_gather_{l}")
        hid = _expert_up(steps_up, xs, w_exp_gate, w_exp_up, l, tm=EXPERT_ROW_TILE,
                         tn=d_exp // up_chunks, name=f"expert_up_{l}")
        y = _expert_down(steps_down, hid, w_exp_down, l, tm=EXPERT_ROW_TILE,
                         tn=d_model // down_chunks, name=f"expert_down_{l}")
        x, xb = _combine(pos1, pos2, y, x1, g1, g2, ln2_g3, ln2_b3, l, alpha,
                         tm=_row_tile(m, 128), name=f"moe_combine_{l}")

        ckv_p.append(ckv[:n_prompt].reshape(batch, seq, kv_rank))
        kr_p.append(kr[:n_prompt].reshape(batch, seq, rope_dim))
        ckv_d.append(ckv[n_prompt:].reshape(n_seq, 1, kv_rank))
        kr_d.append(kr[n_prompt:].reshape(n_seq, 1, rope_dim))
        v_d.append(v[n_prompt:].reshape(n_seq, 1, d_sgu))

    return (x[:n_prompt].reshape(batch, seq, d_model), x[n_prompt:].reshape(n_seq, 1, d_model),
            jnp.stack(ckv_p), jnp.stack(kr_p), jnp.stack(ckv_d), jnp.stack(kr_d), jnp.stack(v_d))
```

```python
import functools

import jax
import jax.numpy as jnp
from jax import lax
from jax.experimental import pallas as pl
from jax.experimental.pallas import tpu as pltpu

F32 = jnp.float32
BF16 = jnp.bfloat16

ROPE_THETA = 10000.0
LN_EPS = 1e-5
RMS_EPS = 1e-6
N_EXPERT_GROUPS = 4

LANES = 128
SUBLANES = 8
VMEM_LIMIT_BYTES = 52 * 1024 * 1024

ROW_TILE = 640
COL_TILE = 1024
EXPERT_ROW_TILE = 128
EXPERT_COL_TILE = 512
DOWN_COL_TILE = 2048
ATTN_Q_TILE = 512
GATHER_UNROLL = 8
GATHER_SLOTS = 6
COMBINE_SLOTS = 3


def _params(semantics):
    return pltpu.CompilerParams(dimension_semantics=semantics, vmem_limit_bytes=VMEM_LIMIT_BYTES)


def _row_tile(m, target):
    best = None
    for d in range(SUBLANES, min(m, target) + 1, SUBLANES):
        if m % d == 0:
            best = d
    return best if best is not None else m


def _col_tile(n, target):
    best = None
    for d in range(LANES, min(n, target) + 1, LANES):
        if n % d == 0:
            best = d
    return best if best is not None else n


def _mm_body(*refs, n_extra, n_out, epi, w_transposed):
    x_ref, w_ref = refs[0], refs[1]
    extra = refs[2:2 + n_extra]
    outs = refs[2 + n_extra:2 + n_extra + n_out]
    if w_transposed:
        acc = lax.dot_general(x_ref[...], w_ref[...], (((1,), (1,)), ((), ())),
                              preferred_element_type=F32)
    else:
        acc = jnp.dot(x_ref[...], w_ref[...], preferred_element_type=F32)
    res = epi(acc, *[e[...] for e in extra])
    if not isinstance(res, tuple):
        res = (res,)
    for o_ref, r in zip(outs, res):
        o_ref[...] = r.astype(o_ref.dtype)


def _matmul(x, w, layer, *, tm, tn, epi, extras=(), outs, name, w_transposed=False):
    m, k = x.shape
    n = w.shape[1] if w_transposed else w.shape[2]
    grid = (m // tm, n // tn)
    if w_transposed:
        w_spec = pl.BlockSpec((None, tn, k), lambda i, j: (layer, j, 0))
    else:
        w_spec = pl.BlockSpec((None, k, tn), lambda i, j: (layer, 0, j))
    in_specs = [pl.BlockSpec((tm, k), lambda i, j: (i, 0)), w_spec]
    in_specs += [pl.BlockSpec(bs, im) for (_, bs, im) in extras]
    out_specs = [pl.BlockSpec((tm, ct), lambda i, j: (i, j)) for (_, ct, _) in outs]
    out_shape = [jax.ShapeDtypeStruct((m, c), dt) for (c, _, dt) in outs]
    body = functools.partial(_mm_body, n_extra=len(extras), n_out=len(outs), epi=epi,
                             w_transposed=w_transposed)
    res = pl.pallas_call(
        body, grid=grid, in_specs=in_specs, out_specs=out_specs, out_shape=out_shape,
        compiler_params=_params(("parallel", "arbitrary")), name=name,
    )(x, w, *[a for (a, _, _) in extras])
    return res


def _rope_pair(t, cs):
    tt = t * cs
    return tt + pltpu.roll(tt, LANES // 2, 1)


def _epi_small(acc, cs, gq, gkv, *, q_rank, kv_rank):
    qa = acc[:, :q_rank]
    ca = acc[:, q_rank:q_rank + kv_rank]
    t = acc[:, q_rank + kv_rank:]
    qn = qa * lax.rsqrt(jnp.mean(qa * qa, axis=-1, keepdims=True) + RMS_EPS) * gq
    ckv = ca * lax.rsqrt(jnp.mean(ca * ca, axis=-1, keepdims=True) + RMS_EPS) * gkv
    r = _rope_pair(t, cs)
    lane = lax.broadcasted_iota(jnp.int32, r.shape, 1)
    kr_pad = jnp.where(lane < LANES // 2, r, 0.0)
    return qn, ckv, r[:, :LANES // 2], kr_pad


def _epi_q(acc, cs, *, heads, scale):
    parts = []
    for h in range(heads):
        blk = acc[:, h * 2 * LANES:(h + 1) * 2 * LANES]
        parts.append(blk[:, :LANES] * scale)
        parts.append(_rope_pair(blk[:, LANES:], cs) * scale)
    return jnp.concatenate(parts, axis=1)


def _epi_gelu(acc):
    return jax.nn.gelu(acc)


def _epi_sigmoid(acc):
    return jax.nn.sigmoid(acc)


def _epi_id(acc):
    return acc


def _ln(x, g, b):
    mu = jnp.mean(x, axis=-1, keepdims=True)
    xc = x - mu
    var = jnp.mean(xc * xc, axis=-1, keepdims=True)
    return xc * lax.rsqrt(var + LN_EPS) * g + b


def _ln_rows_body(x_ref, g_ref, b_ref, o_ref):
    o_ref[...] = _ln(x_ref[...], g_ref[...], b_ref[...])


def _ln_rows(x, g, b, layer, *, tm, name):
    m, n = x.shape
    return pl.pallas_call(
        _ln_rows_body, grid=(m // tm,),
        in_specs=[pl.BlockSpec((tm, n), lambda i: (i, 0)),
                  pl.BlockSpec((None, 1, n), lambda i: (layer, 0, 0)),
                  pl.BlockSpec((None, 1, n), lambda i: (layer, 0, 0))],
        out_specs=pl.BlockSpec((tm, n), lambda i: (i, 0)),
        out_shape=jax.ShapeDtypeStruct((m, n), F32),
        compiler_params=_params(("parallel",)), name=name,
    )(x, g, b)


def _ln_residual_body(x_ref, y_ref, g_ref, b_ref, o_ref, ob_ref, *, alpha):
    r = _ln(alpha * x_ref[...] + y_ref[...], g_ref[...], b_ref[...])
    o_ref[...] = r
    ob_ref[...] = r.astype(BF16)


def _ln_residual(x, y, g, b, layer, alpha, *, tm, name):
    m, n = x.shape
    return pl.pallas_call(
        functools.partial(_ln_residual_body, alpha=alpha), grid=(m // tm,),
        in_specs=[pl.BlockSpec((tm, n), lambda i: (i, 0)),
                  pl.BlockSpec((tm, n), lambda i: (i, 0)),
                  pl.BlockSpec((None, 1, n), lambda i: (layer, 0, 0)),
                  pl.BlockSpec((None, 1, n), lambda i: (layer, 0, 0))],
        out_specs=[pl.BlockSpec((tm, n), lambda i: (i, 0)),
                   pl.BlockSpec((tm, n), lambda i: (i, 0))],
        out_shape=[jax.ShapeDtypeStruct((m, n), F32), jax.ShapeDtypeStruct((m, n), BF16)],
        compiler_params=_params(("parallel",)), name=name,
    )(x, y, g, b)


def _kv_expand_body(c_ref, kr_ref, wk_ref, wv_ref, k_ref, v_ref, *, heads):
    c = c_ref[...].astype(BF16)
    kn = jnp.dot(c, wk_ref[...], preferred_element_type=F32).astype(BF16)
    kr = kr_ref[...]
    for h in range(heads):
        k_ref[:, h * 2 * LANES:h * 2 * LANES + LANES] = kn[:, h * LANES:(h + 1) * LANES]
        k_ref[:, h * 2 * LANES + LANES:(h + 1) * 2 * LANES] = kr
    v_ref[...] = jnp.dot(c, wv_ref[...], preferred_element_type=F32).astype(BF16)


def _kv_expand(ckv, kr_pad, wk, wv, layer, n_rows, *, tm, heads_per_step, name):
    kv_rank = ckv.shape[1]
    n_heads = wk.shape[-1] // LANES
    hb = heads_per_step
    return pl.pallas_call(
        functools.partial(_kv_expand_body, heads=hb), grid=(n_rows // tm, n_heads // hb),
        in_specs=[pl.BlockSpec((tm, kv_rank), lambda i, j: (i, 0)),
                  pl.BlockSpec((tm, LANES), lambda i, j: (i, 0)),
                  pl.BlockSpec((None, kv_rank, hb * LANES), lambda i, j: (layer, 0, j)),
                  pl.BlockSpec((None, kv_rank, hb * LANES), lambda i, j: (layer, 0, j))],
        out_specs=[pl.BlockSpec((tm, hb * 2 * LANES), lambda i, j: (i, j)),
                   pl.BlockSpec((tm, hb * LANES), lambda i, j: (i, j))],
        out_shape=[jax.ShapeDtypeStruct((n_rows, n_heads * 2 * LANES), BF16),
                   jax.ShapeDtypeStruct((n_rows, n_heads * LANES), BF16)],
        compiler_params=_params(("parallel", "arbitrary")), name=name,
    )(ckv, kr_pad, wk, wv)


def _prompt_attn_body(q_ref, k_ref, v_ref, o_ref, *, tq, seq):
    nt = (((1,), (1,)), ((), ()))
    row = lax.broadcasted_iota(jnp.int32, (tq, tq), 0)
    col = lax.broadcasted_iota(jnp.int32, (tq, tq), 1)
    causal = col <= row
    for qi in range(seq // tq):
        off, kend = qi * tq, (qi + 1) * tq
        q = q_ref[off:kend, :]
        sd = lax.dot_general(q, k_ref[off:kend, :], nt, preferred_element_type=F32)
        sd = jnp.where(causal, sd, -jnp.inf)
        m = jnp.max(sd, axis=-1, keepdims=True)
        if qi > 0:
            so = lax.dot_general(q, k_ref[0:off, :], nt, preferred_element_type=F32)
            m = jnp.maximum(m, jnp.max(so, axis=-1, keepdims=True))
            po = jnp.exp(so - m)
        pd = jnp.exp(sd - m)
        l = jnp.sum(pd, axis=-1, keepdims=True)
        o = jnp.dot(pd.astype(BF16), v_ref[off:kend, :], preferred_element_type=F32)
        if qi > 0:
            l = l + jnp.sum(po, axis=-1, keepdims=True)
            o = o + jnp.dot(po.astype(BF16), v_ref[0:off, :], preferred_element_type=F32)
        o_ref[off:kend, :] = (o / l).astype(o_ref.dtype)


def _prompt_attn(q, k, v, batch, seq, *, tq, name):
    n_heads = v.shape[1] // LANES
    return pl.pallas_call(
        functools.partial(_prompt_attn_body, tq=tq, seq=seq), grid=(batch, n_heads),
        in_specs=[pl.BlockSpec((seq, 2 * LANES), lambda b, h: (b, h)),
                  pl.BlockSpec((seq, 2 * LANES), lambda b, h: (b, h)),
                  pl.BlockSpec((seq, LANES), lambda b, h: (b, h))],
        out_specs=pl.BlockSpec((seq, LANES), lambda b, h: (b, h)),
        out_shape=jax.ShapeDtypeStruct((batch * seq, n_heads * LANES), BF16),
        compiler_params=_params(("parallel", "parallel")), name=name,
    )(q, k, v)


def _headwise_body(x_ref, w_ref, o_ref, *, heads, in_stride, d_in, d_out):
    for h in range(heads):
        xh = x_ref[:, h * in_stride:h * in_stride + d_in]
        o_ref[:, h * d_out:(h + 1) * d_out] = jnp.dot(
            xh, w_ref[h], preferred_element_type=F32).astype(o_ref.dtype)


def _headwise(x, w, layer, *, in_stride, heads_per_step, name):
    m = x.shape[0]
    n_heads, d_in, d_out = w.shape[1:]
    hb = heads_per_step
    return pl.pallas_call(
        functools.partial(_headwise_body, heads=hb, in_stride=in_stride, d_in=d_in, d_out=d_out),
        grid=(n_heads // hb,),
        in_specs=[pl.BlockSpec((m, hb * in_stride), lambda j: (0, j)),
                  pl.BlockSpec((None, hb, d_in, d_out), lambda j: (layer, j, 0, 0))],
        out_specs=pl.BlockSpec((m, hb * d_out), lambda j: (0, j)),
        out_shape=jax.ShapeDtypeStruct((m, n_heads * d_out), BF16),
        compiler_params=_params(("parallel",)), name=name,
    )(x, w)


def _paged_attn_body(pt_ref, ql_ref, qr_ref, cn_ref, rn_ref, c_hbm, r_hbm, o_ref,
                     cbuf, rbuf, sem, *, n_pages, rope_dim, layer):
    b = pl.program_id(0)
    slot = b % 2

    def page_copies(seq, i, s):
        pg = pt_ref[seq, i]
        return (pltpu.make_async_copy(c_hbm.at[layer, pg], cbuf.at[s, i], sem.at[s]),
                pltpu.make_async_copy(r_hbm.at[layer, pg], rbuf.at[s, i], sem.at[s]))

    def issue(seq, s):
        def body(i, c):
            for cp in page_copies(seq, i, s):
                cp.start()
            return c

        lax.fori_loop(0, n_pages, body, 0, unroll=GATHER_UNROLL)

    @pl.when(b == 0)
    def _():
        issue(0, 0)

    def wait(i, c):
        for cp in page_copies(b, i, slot):
            cp.wait()
        return c

    lax.fori_loop(0, n_pages, wait, 0, unroll=GATHER_UNROLL)

    @pl.when(b + 1 < pl.num_programs(0))
    def _():
        issue(b + 1, 1 - slot)

    ql = ql_ref[...]
    qr = qr_ref[...][:, :rope_dim]
    nt = (((1,), (1,)), ((), ()))
    ss = []
    for i in range(n_pages):
        c = cbuf[slot, i].astype(BF16)
        r_t = rbuf[slot, i].astype(BF16)
        ss.append(lax.dot_general(ql, c, nt, preferred_element_type=F32)
                  + jnp.dot(qr, r_t, preferred_element_type=F32))
    s = jnp.concatenate(ss, axis=1)
    cn = cn_ref[...]
    rn = rn_ref[...]
    s_self = (jnp.sum(ql.astype(F32) * cn, axis=-1, keepdims=True)
              + jnp.sum(qr.astype(F32) * rn, axis=-1, keepdims=True))
    m = jnp.maximum(jnp.max(s, axis=-1, keepdims=True), s_self)
    p = jnp.exp(s - m)
    p_self = jnp.exp(s_self - m)
    l = jnp.sum(p, axis=-1, keepdims=True) + p_self
    pv = p_self * cn
    for i in range(n_pages):
        pv = pv + jnp.dot(p[:, i * LANES:(i + 1) * LANES].astype(BF16),
                          cbuf[slot, i].astype(BF16), preferred_element_type=F32)
    o_ref[...] = (pv / l).astype(o_ref.dtype)


def _paged_attn(page_table, q_lat, q_rope, c_new, r_new, cache_c, cache_r_t, layer, *, name):
    n_seq, n_heads, kv_rank = q_lat.shape
    n_pages = page_table.shape[1]
    page = cache_c.shape[2]
    rope_dim = cache_r_t.shape[2]
    assert page == LANES
    seq_map = lambda b, pt: (b, 0, 0)
    grid_spec = pltpu.PrefetchScalarGridSpec(
        num_scalar_prefetch=1, grid=(n_seq,),
        in_specs=[pl.BlockSpec((None, n_heads, kv_rank), seq_map),
                  pl.BlockSpec((None, n_heads, LANES), seq_map),
                  pl.BlockSpec((None, 1, kv_rank), seq_map),
                  pl.BlockSpec((None, 1, rope_dim), seq_map),
                  pl.BlockSpec(memory_space=pl.ANY),
                  pl.BlockSpec(memory_space=pl.ANY)],
        out_specs=pl.BlockSpec((None, n_heads, kv_rank), seq_map),
        scratch_shapes=[pltpu.VMEM((2, n_pages, page, kv_rank), cache_c.dtype),
                        pltpu.VMEM((2, n_pages, rope_dim, page), cache_r_t.dtype),
                        pltpu.SemaphoreType.DMA((2,))])
    return pl.pallas_call(
        functools.partial(_paged_attn_body, n_pages=n_pages, rope_dim=rope_dim, layer=layer),
        grid_spec=grid_spec,
        out_shape=jax.ShapeDtypeStruct((n_seq, n_heads, kv_rank), BF16),
        compiler_params=_params(("arbitrary",)), name=name,
    )(page_table, q_lat, q_rope, c_new, r_new, cache_c, cache_r_t)


def _sgu_prompt_body(u_ref, v_ref, w_ref, bt_ref, o_ref, *, groups):
    chunk = w_ref.shape[-1]
    row = lax.broadcasted_iota(jnp.int32, (chunk, chunk), 0)
    col = lax.broadcasted_iota(jnp.int32, (chunk, chunk), 1)
    tril = col <= row
    for g in range(groups):
        w = jnp.where(tril, w_ref[g], 0.0).astype(BF16)
        vg = v_ref[:, g * LANES:(g + 1) * LANES].astype(BF16)
        mix = jnp.dot(w, vg, preferred_element_type=F32) + bt_ref[:, g:g + 1]
        ug = u_ref[:, g * LANES:(g + 1) * LANES].astype(F32)
        o_ref[:, g * LANES:(g + 1) * LANES] = (ug * mix).astype(o_ref.dtype)


def _sgu_prompt(u, v, w_s, b_s_t, layer, n_rows, *, name):
    groups, chunk = w_s.shape[1], w_s.shape[2]
    d_sgu = u.shape[1]
    return pl.pallas_call(
        functools.partial(_sgu_prompt_body, groups=groups), grid=(n_rows // chunk,),
        in_specs=[pl.BlockSpec((chunk, d_sgu), lambda i: (i, 0)),
                  pl.BlockSpec((chunk, d_sgu), lambda i: (i, 0)),
                  pl.BlockSpec((None, groups, chunk, chunk), lambda i: (layer, 0, 0, 0)),
                  pl.BlockSpec((None, chunk, groups), lambda i: (layer, 0, 0))],
        out_specs=pl.BlockSpec((chunk, d_sgu), lambda i: (i, 0)),
        out_shape=jax.ShapeDtypeStruct((n_rows, d_sgu), BF16),
        compiler_params=_params(("parallel",)), name=name,
    )(u, v, w_s, b_s_t)


def _sgu_sample_body(u_ref, v_ref, w_ref, b_ref, o_ref):
    mix = v_ref[...] * w_ref[...] + b_ref[...]
    o_ref[...] = (u_ref[...].astype(F32) * mix).astype(o_ref.dtype)


def _sgu_sample(u, v, w_row, b_row, layer, *, name):
    m, d_sgu = u.shape
    return pl.pallas_call(
        _sgu_sample_body, grid=(1,),
        in_specs=[pl.BlockSpec((m, d_sgu), lambda i: (0, 0)),
                  pl.BlockSpec((m, d_sgu), lambda i: (0, 0)),
                  pl.BlockSpec((None, 1, d_sgu), lambda i: (layer, 0, 0)),
                  pl.BlockSpec((None, 1, d_sgu), lambda i: (layer, 0, 0))],
        out_specs=pl.BlockSpec((m, d_sgu), lambda i: (0, 0)),
        out_shape=jax.ShapeDtypeStruct((m, d_sgu), BF16),
        compiler_params=_params(("arbitrary",)), name=name,
    )(u, v, w_row, b_row)


def _gate_merge_body(o_ref, ub_ref, wa_ref, wb_ref, ga_ref, gb_ref, h_ref):
    ya = jnp.dot(o_ref[...], wa_ref[...], preferred_element_type=F32)
    yb = jnp.dot(ub_ref[...], wb_ref[...], preferred_element_type=F32)
    h_ref[...] = (ga_ref[...].astype(F32) * ya + gb_ref[...].astype(F32) * yb).astype(h_ref.dtype)


def _gate_merge(o, ub, w_oa, w_ob, gates, layer, *, tm, tn, name):
    m, ka = o.shape
    kb = ub.shape[1]
    n = w_oa.shape[-1]
    nj = n // tn
    return pl.pallas_call(
        _gate_merge_body, grid=(m // tm, nj),
        in_specs=[pl.BlockSpec((tm, ka), lambda i, j: (i, 0)),
                  pl.BlockSpec((tm, kb), lambda i, j: (i, 0)),
                  pl.BlockSpec((None, ka, tn), lambda i, j: (layer, 0, j)),
                  pl.BlockSpec((None, kb, tn), lambda i, j: (layer, 0, j)),
                  pl.BlockSpec((tm, tn), lambda i, j: (i, j)),
                  pl.BlockSpec((tm, tn), lambda i, j: (i, j + nj))],
        out_specs=pl.BlockSpec((tm, tn), lambda i, j: (i, j)),
        out_shape=jax.ShapeDtypeStruct((m, n), BF16),
        compiler_params=_params(("parallel", "arbitrary")), name=name,
    )(o, ub, w_oa, w_ob, gates, gates)


def _router_body(x_ref, w_ref, b_ref, e1_ref, e2_ref, g1_ref, g2_ref, *, n_experts):
    logits = jnp.dot(x_ref[...], w_ref[...].astype(BF16), preferred_element_type=F32)
    scores = jax.nn.sigmoid(logits)
    sel = scores + b_ref[...]
    per_group = n_experts // N_EXPERT_GROUPS
    lane = lax.broadcasted_iota(jnp.int32, sel.shape, 1).astype(F32)
    neg = -jnp.inf

    def top2(vals):
        m1 = jnp.max(vals, axis=-1, keepdims=True)
        i1 = jnp.min(jnp.where(vals == m1, lane, float(n_experts)), axis=-1, keepdims=True)
        rest = jnp.where(lane == i1, neg, vals)
        m2 = jnp.max(rest, axis=-1, keepdims=True)
        i2 = jnp.min(jnp.where(rest == m2, lane, float(n_experts)), axis=-1, keepdims=True)
        return m1, i1, m2, i2

    best = None
    best_g = None
    for g in range(N_EXPERT_GROUPS):
        in_g = (lane >= g * per_group) & (lane < (g + 1) * per_group)
        m1, _, m2, _ = top2(jnp.where(in_g, sel, neg))
        val = m1 + m2
        if g == 0:
            best, best_g = val, jnp.zeros(val.shape, F32)
        else:
            upd = val > best
            best_g = jnp.where(upd, float(g), best_g)
            best = jnp.where(upd, val, best)
    lo = best_g * per_group
    in_best = (lane >= lo) & (lane < lo + per_group)
    _, e1, _, e2 = top2(jnp.where(in_best, sel, neg))
    w1 = jnp.sum(jnp.where(lane == e1, scores, 0.0), axis=-1, keepdims=True)
    w2 = jnp.sum(jnp.where(lane == e2, scores, 0.0), axis=-1, keepdims=True)
    tot = w1 + w2
    e1_ref[...] = e1.astype(jnp.int32)
    e2_ref[...] = e2.astype(jnp.int32)
    g1_ref[...] = w1 / tot
    g2_ref[...] = w2 / tot


def _router(xb, w_router, b_router, *, tm, name):
    m, d = xb.shape
    n_experts = w_router.shape[1]
    col = pl.BlockSpec((tm, 1), lambda i: (i, 0))
    return pl.pallas_call(
        functools.partial(_router_body, n_experts=n_experts), grid=(m // tm,),
        in_specs=[pl.BlockSpec((tm, d), lambda i: (i, 0)),
                  pl.BlockSpec((d, n_experts), lambda i: (0, 0)),
                  pl.BlockSpec((1, n_experts), lambda i: (0, 0))],
        out_specs=[col, col, col, col],
        out_shape=[jax.ShapeDtypeStruct((m, 1), jnp.int32), jax.ShapeDtypeStruct((m, 1), jnp.int32),
                   jax.ShapeDtypeStruct((m, 1), F32), jax.ShapeDtypeStruct((m, 1), F32)],
        compiler_params=_params(("parallel",)), name=name,
    )(xb, w_router, b_router.reshape(1, n_experts))


def _row_copy(src_hbm, row, dst, i, sem):
    return pltpu.make_async_copy(src_hbm.at[pl.ds(row, 1)], dst.at[pl.ds(i, 1)], sem)


def _tile_ring(step, n_steps, slots, issue):
    ahead = slots - 1

    @pl.when(step == 0)
    def _():
        for t in range(min(ahead, n_steps)):
            issue(t, t)

    @pl.when(step + ahead < n_steps)
    def _():
        issue(step + ahead, lax.rem(step + ahead, slots))

    return lax.rem(step, slots)


def _gather_rows_body(tok_ref, x_hbm, o_ref, buf, sem, *, tm, n_steps, slots):
    step = pl.program_id(0)

    def issue(tile, slot):
        base = tile * tm

        def start(i, c):
            _row_copy(x_hbm, tok_ref[base + 2 * i], buf.at[slot], 2 * i,
                      sem.at[slot]).start(priority=0)
            _row_copy(x_hbm, tok_ref[base + 2 * i + 1], buf.at[slot], 2 * i + 1,
                      sem.at[slot]).start(priority=1)
            return c

        lax.fori_loop(0, tm // 2, start, 0, unroll=GATHER_UNROLL // 2)

    slot = _tile_ring(step, n_steps, slots, issue)

    def wait(i, c):
        _row_copy(x_hbm, 0, buf.at[slot], i, sem.at[slot]).wait()
        return c

    lax.fori_loop(0, tm, wait, 0, unroll=GATHER_UNROLL)
    o_ref[...] = buf[slot].astype(o_ref.dtype)


def _gather_rows(x, row_token, *, tm, name):
    d = x.shape[1]
    n_rows = row_token.shape[0]
    n_steps = n_rows // tm
    slots = GATHER_SLOTS
    grid_spec = pltpu.PrefetchScalarGridSpec(
        num_scalar_prefetch=1, grid=(n_steps,),
        in_specs=[pl.BlockSpec(memory_space=pl.ANY)],
        out_specs=pl.BlockSpec((tm, d), lambda i, tok: (i, 0)),
        scratch_shapes=[pltpu.VMEM((slots, tm, d), x.dtype), pltpu.SemaphoreType.DMA((slots,))])
    return pl.pallas_call(
        functools.partial(_gather_rows_body, tm=tm, n_steps=n_steps, slots=slots),
        grid_spec=grid_spec,
        out_shape=jax.ShapeDtypeStruct((n_rows, d), BF16),
        compiler_params=_params(("arbitrary",)), name=name,
    )(row_token, x)


def _weight_ring(s, exp_ref, wchunk_ref, grp_ref, first_ref, more_ref, nexp_ref, nchunk_ref,
                 copies):
    slot = grp_ref[s] % 2

    @pl.when(s == 0)
    def _():
        for cp in copies(exp_ref[0], wchunk_ref[0], 0):
            cp.start()

    @pl.when(first_ref[s] == 1)
    def _():
        for cp in copies(exp_ref[s], wchunk_ref[s], slot):
            cp.wait()

        @pl.when(more_ref[s] == 1)
        def _():
            for cp in copies(nexp_ref[s], nchunk_ref[s], 1 - slot):
                cp.start()

    return slot


def _expert_up_body(tile_ref, exp_ref, wchunk_ref, ochunk_ref, valid_ref, grp_ref, first_ref,
                    more_ref, nexp_ref, nchunk_ref, x_ref, wg_hbm, wu_hbm, h_ref,
                    wg_buf, wu_buf, sem, *, layer, tn):
    s = pl.program_id(0)

    def copies(e, c, slot):
        cols = pl.ds(pl.multiple_of(c * tn, tn), tn)
        return (pltpu.make_async_copy(wg_hbm.at[layer, e, :, cols], wg_buf.at[slot], sem.at[slot]),
                pltpu.make_async_copy(wu_hbm.at[layer, e, :, cols], wu_buf.at[slot], sem.at[slot]))

    slot = _weight_ring(s, exp_ref, wchunk_ref, grp_ref, first_ref, more_ref, nexp_ref,
                        nchunk_ref, copies)

    @pl.when(valid_ref[s] == 1)
    def _():
        x = x_ref[...].astype(F32)
        g = jnp.dot(x, wg_buf[slot], preferred_element_type=F32)
        u = jnp.dot(x, wu_buf[slot], preferred_element_type=F32)
        h_ref[...] = (jax.nn.silu(g) * u).astype(h_ref.dtype)

    @pl.when(valid_ref[s] == 0)
    def _():
        h_ref[...] = jnp.zeros(h_ref.shape, h_ref.dtype)


def _step_maps():
    x_map = lambda s, tile, exp, wchunk, ochunk, *_: (tile[s], 0)
    o_map = lambda s, tile, exp, wchunk, ochunk, *_: (tile[s], ochunk[s])
    return x_map, o_map


def _expert_up(steps, xs, w_gate, w_up, layer, *, tm, tn, name):
    n_rows, d = xs.shape
    d_exp = w_gate.shape[-1]
    x_map, o_map = _step_maps()
    grid_spec = pltpu.PrefetchScalarGridSpec(
        num_scalar_prefetch=len(steps), grid=(steps[0].shape[0],),
        in_specs=[pl.BlockSpec((tm, d), x_map),
                  pl.BlockSpec(memory_space=pl.ANY),
                  pl.BlockSpec(memory_space=pl.ANY)],
        out_specs=pl.BlockSpec((tm, tn), o_map),
        scratch_shapes=[pltpu.VMEM((2, d, tn), w_gate.dtype), pltpu.VMEM((2, d, tn), w_up.dtype),
                        pltpu.SemaphoreType.DMA((2,))])
    return pl.pallas_call(
        functools.partial(_expert_up_body, layer=layer, tn=tn), grid_spec=grid_spec,
        out_shape=jax.ShapeDtypeStruct((n_rows, d_exp), BF16),
        compiler_params=_params(("arbitrary",)), name=name,
    )(*steps, xs, w_gate, w_up)


def _expert_down_body(tile_ref, exp_ref, wchunk_ref, ochunk_ref, valid_ref, grp_ref, first_ref,
                      more_ref, nexp_ref, nchunk_ref, h_ref, wd_hbm, y_ref, wd_buf, sem,
                      *, layer, tn):
    s = pl.program_id(0)

    def copies(e, c, slot):
        cols = pl.ds(pl.multiple_of(c * tn, tn), tn)
        return (pltpu.make_async_copy(wd_hbm.at[layer, e, :, cols], wd_buf.at[slot], sem.at[slot]),)

    slot = _weight_ring(s, exp_ref, wchunk_ref, grp_ref, first_ref, more_ref, nexp_ref,
                        nchunk_ref, copies)

    @pl.when(valid_ref[s] == 1)
    def _():
        y_ref[...] = jnp.dot(h_ref[...].astype(F32), wd_buf[slot],
                             preferred_element_type=F32).astype(y_ref.dtype)

    @pl.when(valid_ref[s] == 0)
    def _():
        y_ref[...] = jnp.zeros(y_ref.shape, y_ref.dtype)


def _expert_down(steps, h, w_down, layer, *, tm, tn, name):
    n_rows, d_exp = h.shape
    d = w_down.shape[-1]
    x_map, o_map = _step_maps()
    grid_spec = pltpu.PrefetchScalarGridSpec(
        num_scalar_prefetch=len(steps), grid=(steps[0].shape[0],),
        in_specs=[pl.BlockSpec((tm, d_exp), x_map),
                  pl.BlockSpec(memory_space=pl.ANY)],
        out_specs=pl.BlockSpec((tm, tn), o_map),
        scratch_shapes=[pltpu.VMEM((2, d_exp, tn), w_down.dtype), pltpu.SemaphoreType.DMA((2,))])
    return pl.pallas_call(
        functools.partial(_expert_down_body, layer=layer, tn=tn), grid_spec=grid_spec,
        out_shape=jax.ShapeDtypeStruct((n_rows, d), F32),
        compiler_params=_params(("arbitrary",)), name=name,
    )(*steps, h, w_down)


def _combine_body(p1_ref, p2_ref, y_hbm, x_ref, g1_ref, g2_ref, lg_ref, lb_ref, o_ref, ob_ref,
                  buf1, buf2, sem, *, tm, alpha, n_steps, slots):
    step = pl.program_id(0)

    def issue(tile, slot):
        base = tile * tm

        def start(i, c):
            _row_copy(y_hbm, p1_ref[base + i], buf1.at[slot], i, sem.at[slot]).start(priority=0)
            _row_copy(y_hbm, p2_ref[base + i], buf2.at[slot], i, sem.at[slot]).start(priority=1)
            return c

        lax.fori_loop(0, tm, start, 0, unroll=GATHER_UNROLL)

    slot = _tile_ring(step, n_steps, slots, issue)

    def wait(i, c):
        _row_copy(y_hbm, 0, buf1.at[slot], i, sem.at[slot]).wait()
        _row_copy(y_hbm, 0, buf2.at[slot], i, sem.at[slot]).wait()
        return c

    lax.fori_loop(0, tm, wait, 0, unroll=GATHER_UNROLL)
    moe = g1_ref[...] * buf1[slot] + g2_ref[...] * buf2[slot]
    r = _ln(alpha * x_ref[...] + moe, lg_ref[...], lb_ref[...])
    o_ref[...] = r
    ob_ref[...] = r.astype(BF16)


def _combine(pos1, pos2, y, x, g1, g2, ln_g, ln_b, layer, alpha, *, tm, name):
    m, d = x.shape
    row = lambda i, p1, p2: (i, 0)
    n_steps = m // tm
    slots = COMBINE_SLOTS
    grid_spec = pltpu.PrefetchScalarGridSpec(
        num_scalar_prefetch=2, grid=(n_steps,),
        in_specs=[pl.BlockSpec(memory_space=pl.ANY),
                  pl.BlockSpec((tm, d), row),
                  pl.BlockSpec((tm, 1), row),
                  pl.BlockSpec((tm, 1), row),
                  pl.BlockSpec((None, 1, d), lambda i, p1, p2: (layer, 0, 0)),
                  pl.BlockSpec((None, 1, d), lambda i, p1, p2: (layer, 0, 0))],
        out_specs=[pl.BlockSpec((tm, d), row), pl.BlockSpec((tm, d), row)],
        scratch_shapes=[pltpu.VMEM((slots, tm, d), y.dtype), pltpu.VMEM((slots, tm, d), y.dtype),
                        pltpu.SemaphoreType.DMA((slots,))])
    return pl.pallas_call(
        functools.partial(_combine_body, tm=tm, alpha=alpha, n_steps=n_steps, slots=slots),
        grid_spec=grid_spec,
        out_shape=[jax.ShapeDtypeStruct((m, d), F32), jax.ShapeDtypeStruct((m, d), BF16)],
        compiler_params=_params(("arbitrary",)), name=name,
    )(pos1, pos2, y, x, g1, g2, ln_g, ln_b)


def _dispatch_plan(e1, e2, n_experts, tm, n_chunks_up, n_chunks_down):
    t = e1.shape[0]
    e_flat = jnp.concatenate([e1.reshape(t), e2.reshape(t)])
    onehot = (e_flat[:, None] == jnp.arange(n_experts, dtype=jnp.int32)[None, :]).astype(jnp.int32)
    csum = jnp.cumsum(onehot, axis=0)
    counts = csum[-1]
    rank = jnp.take_along_axis(csum, e_flat[:, None], axis=1)[:, 0] - 1
    tiles_e = (counts + tm - 1) // tm
    tile_end = jnp.cumsum(tiles_e)
    tile_start = tile_end - tiles_e
    n_tiles = tile_end[-1]
    n_tiles_max = (2 * t + tm - 1) // tm + n_experts
    n_rows_max = n_tiles_max * tm
    dest = tile_start[e_flat] * tm + rank
    token = jnp.concatenate([jnp.arange(t, dtype=jnp.int32)] * 2)
    row_token = jnp.zeros((n_rows_max,), jnp.int32).at[dest].set(token)
    owner = jnp.where(tiles_e > 0, jnp.arange(n_experts, dtype=jnp.int32), n_experts)
    next_expert = jnp.concatenate([lax.cummin(owner, axis=0, reverse=True)[1:],
                                   jnp.full((1,), n_experts, jnp.int32)])

    def steps(n_chunks):
        n_steps = n_tiles_max * n_chunks
        n_valid = n_tiles * n_chunks
        s = jnp.arange(n_steps, dtype=jnp.int32)
        valid = s < n_valid
        sc = jnp.minimum(s, n_valid - 1)
        exp = jnp.sum(sc[:, None] >= (tile_end * n_chunks)[None, :], axis=1).astype(jnp.int32)
        exp = jnp.minimum(exp, n_experts - 1)
        local = sc - tile_start[exp] * n_chunks
        nt = jnp.maximum(tiles_e[exp], 1)
        wchunk = local // nt
        tile = tile_start[exp] + local % nt
        pad = s - n_valid
        tile = jnp.where(valid, tile, n_tiles + pad // n_chunks)
        ochunk = jnp.where(valid, wchunk, pad % n_chunks)
        key = exp * n_chunks + wchunk
        first = jnp.concatenate([jnp.ones((1,), jnp.bool_), key[1:] != key[:-1]]) & valid
        grp = jnp.cumsum(first.astype(jnp.int32)) - 1
        last_chunk = wchunk == n_chunks - 1
        nexp = jnp.where(last_chunk, next_expert[exp], exp)
        nchunk = jnp.where(last_chunk, 0, wchunk + 1)
        more = first & (nexp < n_experts)
        nexp = jnp.minimum(nexp, n_experts - 1)
        i32 = lambda a: a.astype(jnp.int32)
        return (i32(tile), i32(exp), i32(wchunk), i32(ochunk), i32(valid), i32(grp), i32(first),
                i32(more), i32(nexp), i32(nchunk))

    return row_token, dest[:t], dest[t:], steps(n_chunks_up), steps(n_chunks_down)


def kernel(x_prompt, x_sample, cache_ckv, cache_krope, page_table, w_in, q_norm_g, kv_norm_g, w_qb, w_kvb, w_oa, sgu_ln_g, sgu_ln_b, w_s, b_s, w_ob, w_out, ln1_g, ln1_b, w_router, b_router, w_exp_gate, w_exp_up, w_exp_down, ln2_g, ln2_b):
    batch, seq, d_model = x_prompt.shape
    n_seq, dec_seq, _ = x_sample.shape
    assert dec_seq == 1
    depth = w_in.shape[0]
    q_rank = q_norm_g.shape[1]
    kv_rank = kv_norm_g.shape[1]
    rope_dim = cache_krope.shape[-1]
    n_heads = w_kvb.shape[2]
    nope_dim = w_qb.shape[2] // n_heads - rope_dim
    v_dim = w_kvb.shape[3] - nope_dim
    d_sgu = sgu_ln_g.shape[1]
    n_groups, chunk = w_s.shape[1], w_s.shape[2]
    n_experts = w_router.shape[1]
    d_exp = w_exp_gate.shape[3]
    n_pages = page_table.shape[1]
    past_len = n_pages * cache_ckv.shape[2]
    assert nope_dim == LANES and v_dim == LANES and 2 * rope_dim == LANES
    assert d_sgu == n_groups * LANES and chunk == LANES and seq % chunk == 0
    half = rope_dim // 2
    alpha = float((2 * depth) ** 0.25)
    scale = float((nope_dim + rope_dim) ** -0.5)

    n_prompt = batch * seq
    m = n_prompt + n_seq
    tm = _row_tile(m, ROW_TILE)

    c0 = q_rank + kv_rank
    c1 = c0 + rope_dim
    c2 = c1 + d_sgu
    c3 = c2 + d_sgu
    w_in_t = jnp.swapaxes(w_in, 1, 2)
    w_small = jnp.concatenate(
        [w_in_t[:, :c1], -w_in_t[:, c0 + half:c1], w_in_t[:, c0:c0 + half]], axis=1).astype(BF16)
    w_u = w_in_t[:, c1:c2].astype(BF16)
    w_v = w_in_t[:, c2:c3].astype(BF16)
    w_g = w_in_t[:, c3:].astype(BF16)
    cache_krope_t = jnp.swapaxes(cache_krope, 2, 3)
    wq = w_qb.reshape(depth, q_rank, n_heads, nope_dim + rope_dim)
    wq_r = wq[..., nope_dim:]
    wq_ext = jnp.concatenate(
        [wq, -wq_r[..., half:], wq_r[..., :half]], axis=-1
    ).reshape(depth, q_rank, n_heads * 2 * LANES).astype(BF16)
    wk_flat = w_kvb[..., :nope_dim].reshape(depth, kv_rank, n_heads * LANES).astype(BF16)
    wv_flat = w_kvb[..., nope_dim:].reshape(depth, kv_rank, n_heads * LANES).astype(BF16)
    wk_t = jnp.transpose(w_kvb[..., :nope_dim], (0, 2, 3, 1)).astype(BF16)
    wv_h = jnp.transpose(w_kvb[..., nope_dim:], (0, 2, 1, 3)).astype(BF16)
    w_oa_b = w_oa.astype(BF16)
    w_ob_b = w_ob.astype(BF16)
    w_out_b = w_out.astype(BF16)
    w_s_first = jnp.repeat(w_s[:, :, 0, 0], LANES, axis=1).reshape(depth, 1, d_sgu)
    b_s_first = jnp.repeat(b_s[:, :, 0], LANES, axis=1).reshape(depth, 1, d_sgu)
    b_s_t = jnp.transpose(b_s, (0, 2, 1))
    g3 = lambda a: a.reshape(depth, 1, a.shape[1])
    q_norm3, kv_norm3 = g3(q_norm_g), g3(kv_norm_g)
    sgu_g3, sgu_b3 = g3(sgu_ln_g), g3(sgu_ln_b)
    ln1_g3, ln1_b3, ln2_g3, ln2_b3 = g3(ln1_g), g3(ln1_b), g3(ln2_g), g3(ln2_b)

    pos = jnp.concatenate([jnp.tile(jnp.arange(seq, dtype=jnp.int32), batch),
                           jnp.full((n_seq,), past_len, jnp.int32)])
    inv = ROPE_THETA ** (-jnp.arange(half, dtype=F32) / half)
    ang = pos.astype(F32)[:, None] * inv[None, :]
    cs = jnp.concatenate([jnp.cos(ang), jnp.cos(ang), jnp.sin(ang), jnp.sin(ang)], axis=1)

    x = jnp.concatenate([x_prompt.reshape(n_prompt, d_model), x_sample.reshape(n_seq, d_model)])
    xb = x.astype(BF16)

    n_small = w_small.shape[1]
    tn_q = _col_tile(wq_ext.shape[-1], COL_TILE)
    heads_q = tn_q // (2 * LANES)
    tn_d = _col_tile(d_model, COL_TILE)
    up_chunks = d_exp // _col_tile(d_exp, EXPERT_COL_TILE)
    down_chunks = d_model // _col_tile(d_model, DOWN_COL_TILE)
    cs_spec = (cs, (tm, LANES), lambda i, j: (i, 0))

    ckv_p, kr_p, ckv_d, kr_d, v_d = [], [], [], [], []
    for l in range(depth):
        qn, ckv, kr, kr_pad = _matmul(
            xb, w_small, l, tm=tm, tn=n_small, w_transposed=True,
            epi=functools.partial(_epi_small, q_rank=q_rank, kv_rank=kv_rank),
            extras=[cs_spec,
                    (q_norm3, (None, 1, q_rank), lambda i, j: (l, 0, 0)),
                    (kv_norm3, (None, 1, kv_rank), lambda i, j: (l, 0, 0))],
            outs=[(q_rank, q_rank, BF16), (kv_rank, kv_rank, F32), (rope_dim, rope_dim, F32),
                  (LANES, LANES, BF16)],
            name=f"in_small_{l}")
        (u,) = _matmul(xb, w_u, l, tm=tm, tn=_col_tile(d_sgu, COL_TILE), epi=_epi_gelu,
                       outs=[(d_sgu, _col_tile(d_sgu, COL_TILE), BF16)], w_transposed=True,
                       name=f"in_u_{l}")
        (v_pre,) = _matmul(xb, w_v, l, tm=tm, tn=_col_tile(d_sgu, COL_TILE), epi=_epi_gelu,
                           outs=[(d_sgu, _col_tile(d_sgu, COL_TILE), F32)], w_transposed=True,
                           name=f"in_v_{l}")
        v = _ln_rows(v_pre, sgu_g3, sgu_b3, l, tm=_row_tile(m, 512), name=f"sgu_ln_{l}")
        (gates,) = _matmul(xb, w_g, l, tm=tm, tn=tn_d, epi=_epi_sigmoid,
                           outs=[(2 * d_model, tn_d, BF16)], w_transposed=True,
                           name=f"in_gates_{l}")

        (q_full,) = _matmul(qn, wq_ext, l, tm=tm, tn=tn_q,
                            epi=functools.partial(_epi_q, heads=heads_q, scale=scale),
                            extras=[cs_spec],
                            outs=[(n_heads * 2 * LANES, tn_q, BF16)], name=f"q_proj_{l}")
        k_full, v_full = _kv_expand(ckv, kr_pad, wk_flat, wv_flat, l, n_prompt,
                                    tm=_row_tile(n_prompt, 1024),
                                    heads_per_step=min(n_heads, 8), name=f"kv_expand_{l}")
        o_prompt = _prompt_attn(q_full, k_full, v_full, batch, seq,
                                tq=min(seq, ATTN_Q_TILE), name=f"prompt_attn_{l}")
        q_dec = q_full[n_prompt:]
        q_lat = _headwise(q_dec, wk_t, l, in_stride=2 * LANES,
                          heads_per_step=min(n_heads, 8), name=f"q_absorb_{l}")
        q_rope_dec = q_dec.reshape(n_seq, n_heads, 2 * LANES)[:, :, LANES:]
        o_lat = _paged_attn(page_table, q_lat.reshape(n_seq, n_heads, kv_rank), q_rope_dec,
                            ckv[n_prompt:].reshape(n_seq, 1, kv_rank),
                            kr[n_prompt:].reshape(n_seq, 1, rope_dim),
                            cache_ckv, cache_krope_t, l, name=f"paged_attn_{l}")
        o_dec = _headwise(o_lat.reshape(n_seq, n_heads * kv_rank), wv_h, l, in_stride=kv_rank,
                          heads_per_step=min(n_heads, 8), name=f"v_up_{l}")
        o = jnp.concatenate([o_prompt, o_dec])

        ub_prompt = _sgu_prompt(u, v, w_s, b_s_t, l, n_prompt, name=f"sgu_prompt_{l}")
        ub_dec = _sgu_sample(u[n_prompt:], v[n_prompt:], w_s_first, b_s_first, l,
                             name=f"sgu_sample_{l}")
        ub = jnp.concatenate([ub_prompt, ub_dec])

        h = _gate_merge(o, ub, w_oa_b, w_ob_b, gates, l, tm=tm, tn=_col_tile(d_model, 512),
                        name=f"gate_merge_{l}")
        (mix,) = _matmul(h, w_out_b, l, tm=tm, tn=tn_d, epi=_epi_id,
                         outs=[(d_model, tn_d, F32)], name=f"out_proj_{l}")
        x1, x1b = _ln_residual(x, mix, ln1_g3, ln1_b3, l, alpha, tm=_row_tile(m, 128),
                               name=f"ln1_{l}")

        e1, e2, g1, g2 = _router(x1b, w_router, b_router, tm=tm, name=f"router_{l}")
        row_token, pos1, pos2, steps_up, steps_down = _dispatch_plan(
            e1, e2, n_experts, EXPERT_ROW_TILE, up_chunks, down_chunks)
        xs = _gather_rows(x1, row_token, tm=EXPERT_ROW_TILE, name=f"moe_gather_{l}")
        hid = _expert_up(steps_up, xs, w_exp_gate, w_exp_up, l, tm=EXPERT_ROW_TILE,
                         tn=d_exp // up_chunks, name=f"expert_up_{l}")
        y = _expert_down(steps_down, hid, w_exp_down, l, tm=EXPERT_ROW_TILE,
                         tn=d_model // down_chunks, name=f"expert_down_{l}")
        x, xb = _combine(pos1, pos2, y, x1, g1, g2, ln2_g3, ln2_b3, l, alpha,
                         tm=_row_tile(m, 128), name=f"moe_combine_{l}")

        ckv_p.append(ckv[:n_prompt].reshape(batch, seq, kv_rank))
        kr_p.append(kr[:n_prompt].reshape(batch, seq, rope_dim))
        ckv_d.append(ckv[n_prompt:].reshape(n_seq, 1, kv_rank))
        kr_d.append(kr[n_prompt:].reshape(n_seq, 1, rope_dim))
        v_d.append(v[n_prompt:].reshape(n_seq, 1, d_sgu))

    return (x[:n_prompt].reshape(batch, seq, d_model), x[n_prompt:].reshape(n_seq, 1, d_model),
            jnp.stack(ckv_p), jnp.stack(kr_p), jnp.stack(ckv_d), jnp.stack(kr_d), jnp.stack(v_d))
```
